```python
import jax, jax.numpy as jnp
from jax import lax
import numpy as np

D_MODEL = 2048
BATCH = 1
SEQ = 8192
DEPTH = 1

GRID_W = 64
CTX_LEN = 256
MIX_WIDTH = D_MODEL
HG_HEAD_DIM = 128
HG_WIDTH = MIX_WIDTH // 2
HG_HEADS = HG_WIDTH // HG_HEAD_DIM
NA_HEAD_DIM = 128
NA_WIDTH = MIX_WIDTH - HG_WIDTH
NA_HEADS = NA_WIDTH // NA_HEAD_DIM
NA_MAX_KH = 8
NA_KW = 16
HG_COLS = 5 * HG_WIDTH
NA_COLS = 3 * NA_WIDTH
IN_COLS = HG_COLS + NA_COLS
D_FF = 256 * ((8 * D_MODEL // 3 + 255) // 256)
HG_CHUNK = 64
ROPE_BASE = 10000.0
N_MOD = 9
EPS = 1e-6

kernel_name = 'hymba_hgrn2_natten_macaron_dit'

F32 = jnp.float32


def rmsnorm(x, w):
    xf = x.astype(F32)
    return xf * lax.rsqrt(jnp.mean(xf * xf, axis=-1, keepdims=True) + EPS) * w.astype(F32)


def modulate(h, shift, scale):
    return h * (1.0 + scale) + shift


def swiglu(h, w_gate, w_up, w_down):
    return (jax.nn.silu(h @ w_gate) * (h @ w_up)) @ w_down


def adaln_half_ffn(h, norm_gain, shift, scale, gate, w_gate, w_up, w_down):
    u = modulate(rmsnorm(h, norm_gain), shift, scale)
    return h + 0.5 * gate * swiglu(u, w_gate, w_up, w_down)


def axial_rope_tables(n_tokens):
    t = jnp.arange(n_tokens)
    row = (t // GRID_W).astype(F32)
    col = (t % GRID_W).astype(F32)
    quarter = NA_HEAD_DIM // 4
    inv = ROPE_BASE ** (-jnp.arange(quarter, dtype=F32) / quarter)
    ang_r = row[:, None] * inv[None, :]
    ang_c = col[:, None] * inv[None, :]
    ang = jnp.concatenate([ang_r, ang_r, ang_c, ang_c], axis=-1)
    return jnp.cos(ang)[:, None, :], jnp.sin(ang)[:, None, :]


def apply_axial_rope(x, cos, sin):
    def rot_half(a):
        a1, a2 = jnp.split(a, 2, axis=-1)
        return jnp.concatenate([-a2, a1], axis=-1)
    xr, xc = jnp.split(x, 2, axis=-1)
    return x * cos + jnp.concatenate([rot_half(xr), rot_half(xc)], axis=-1) * sin


def gla_chunk_scan(q, k, v, log_f, s0):
    B, T, H, DK = q.shape
    DV = v.shape[-1]
    n = T // HG_CHUNK

    def chunks(a):
        return a.reshape(B, n, HG_CHUNK, H, a.shape[-1]).transpose(1, 0, 3, 2, 4)

    lower = jnp.tril(jnp.ones((HG_CHUNK, HG_CHUNK), dtype=bool))[:, :, None]

    def step(S, blk):
        qc, kc, vc, gc = blk
        b = jnp.cumsum(gc, axis=2)
        rel = jnp.exp(jnp.where(lower, b[:, :, :, None, :] - b[:, :, None, :, :], -jnp.inf))
        scores = jnp.einsum('bhtd,bhsd,bhtsd->bhts', qc, kc, rel)
        o = (jnp.einsum('bhts,bhsv->bhtv', scores, vc)
             + jnp.einsum('bhtd,bhdv->bhtv', qc * jnp.exp(b), S))
        b_end = b[:, :, -1:, :]
        S = (jnp.exp(b_end[:, :, 0, :, None]) * S
             + jnp.einsum('bhsd,bhsv->bhdv', kc * jnp.exp(b_end - b), vc))
        return S, o

    s_fin, o = lax.scan(step, s0, (chunks(q), chunks(k), chunks(v), chunks(log_f)))
    return o.transpose(1, 0, 3, 2, 4).reshape(B, T, H, DV), s_fin


def flip_time(a, rev):
    return a[:, ::-1] if rev else a


def hgrn2_group(p_lat, p_ctx, lb, norm_w, need_ctx):
    def prepare(p):
        B, T, _ = p.shape
        q, f_fw, f_bw, i, g = jnp.split(p.astype(F32), 5, axis=-1)
        hd = lambda a: a.reshape(B, T, HG_HEADS, HG_HEAD_DIM)
        dirs = []
        for f_raw, lb_d in ((f_fw, lb[0]), (f_bw, lb[1])):
            f = lb_d + (1.0 - lb_d) * jax.nn.sigmoid(f_raw)
            dirs.append((hd(jnp.log(f)), hd(1.0 - f)))
        return hd(jax.nn.silu(q)), hd(i), g, dirs

    q_l, v_l, g_l, dirs_l = prepare(p_lat)
    q_c, v_c, g_c, dirs_c = prepare(p_ctx)
    B = p_lat.shape[0]
    s0 = jnp.zeros((B, HG_HEADS, HG_HEAD_DIM, HG_HEAD_DIM), F32)
    outs_l, outs_c = [], []
    for (logf_l, k_l), (logf_c, k_c), rev in zip(dirs_l, dirs_c, (False, True)):
        oc, s_ctx = gla_chunk_scan(flip_time(q_c, rev), flip_time(k_c, rev), flip_time(v_c, rev),
                                   flip_time(logf_c, rev), s0)
        ol, _ = gla_chunk_scan(flip_time(q_l, rev), flip_time(k_l, rev), flip_time(v_l, rev),
                               flip_time(logf_l, rev), s_ctx)
        outs_l.append(flip_time(ol, rev))
        outs_c.append(flip_time(oc, rev))

    def readout(o, g):
        Bo, T = o.shape[0], o.shape[1]
        return rmsnorm(o, norm_w).reshape(Bo, T, HG_WIDTH) * jax.nn.silu(g)

    y_l = readout(outs_l[0] + outs_l[1], g_l)
    if not need_ctx:
        return y_l, None
    return y_l, readout(outs_c[0] + outs_c[1], g_c)


def neighbourhood_group(p_lat, p_ctx, rpb, cos, sin, rows, need_ctx):
    B, T, _ = p_lat.shape
    scale = NA_HEAD_DIM ** -0.5
    heads = lambda a: a.reshape(a.shape[0], a.shape[1], NA_HEADS, NA_HEAD_DIM)
    q_l, k_l, v_l = (heads(a) for a in jnp.split(p_lat.astype(F32), 3, axis=-1))
    q_c, k_c, v_c = (heads(a) for a in jnp.split(p_ctx.astype(F32), 3, axis=-1))
    q_l = apply_axial_rope(q_l, cos, sin) * scale
    k_l = apply_axial_rope(k_l, cos, sin)

    kh = min(NA_MAX_KH, rows)
    grid = lambda a: a.reshape(B, rows, GRID_W, NA_HEADS, NA_HEAD_DIM)
    q_g, k_g, v_g = grid(q_l), grid(k_l), grid(v_l)
    r = jnp.arange(rows)
    row_start = jnp.clip(r - kh // 2, 0, rows - kh)
    key_rows = row_start[:, None] + jnp.arange(kh)[None, :]
    k_win = k_g[:, key_rows]
    v_win = v_g[:, key_rows]
    col = jnp.arange(GRID_W)
    col_start = jnp.clip(col - NA_KW // 2, 0, GRID_W - NA_KW)
    col_in = (col[None, :] >= col_start[:, None]) & (col[None, :] < col_start[:, None] + NA_KW)
    dr = key_rows - r[:, None] + (NA_MAX_KH - 1)
    dc = jnp.clip(col[None, :] - col[:, None], 1 - NA_KW, NA_KW - 1) + (NA_KW - 1)
    bias = rpb[:, dr[:, None, :, None], dc[None, :, None, :]].astype(F32)

    s_win = jnp.einsum('brchd,brkjhd->bhrckj', q_g, k_win) + bias
    s_win = jnp.where(col_in[:, None, :], s_win, -jnp.inf)
    n_win = kh * GRID_W
    s_win = s_win.reshape(B, NA_HEADS, rows, GRID_W, n_win)
    s_ctx = jnp.einsum('brchd,bkhd->bhrck', q_g, k_c)
    probs = jax.nn.softmax(jnp.concatenate([s_win, s_ctx], axis=-1), axis=-1)
    p_win = probs[..., :n_win].reshape(B, NA_HEADS, rows, GRID_W, kh, GRID_W)
    o_l = (jnp.einsum('bhrckj,brkjhd->brchd', p_win, v_win)
           + jnp.einsum('bhrck,bkhd->brchd', probs[..., n_win:], v_c))
    o_l = o_l.reshape(B, T, NA_WIDTH)
    if not need_ctx:
        return o_l, None
    p_cc = jax.nn.softmax(jnp.einsum('bqhd,bkhd->bhqk', q_c * scale, k_c), axis=-1)
    o_c = jnp.einsum('bhqk,bkhd->bqhd', p_cc, v_c).reshape(B, p_ctx.shape[1], NA_WIDTH)
    return o_l, o_c


def hybrid_mixer(u_lat, u_ctx, w_in, lb, hg_norm_w, rpb, cos, sin, rows, need_ctx):
    p_lat = u_lat @ w_in
    p_ctx = u_ctx @ w_in
    hg_l, hg_c = hgrn2_group(p_lat[..., :HG_COLS], p_ctx[..., :HG_COLS], lb, hg_norm_w, need_ctx)
    na_l, na_c = neighbourhood_group(p_lat[..., HG_COLS:], p_ctx[..., HG_COLS:], rpb, cos, sin, rows, need_ctx)
    mix_l = jnp.concatenate([hg_l, na_l], axis=-1)
    mix_c = jnp.concatenate([hg_c, na_c], axis=-1) if need_ctx else None
    return mix_l, mix_c


def setup_inputs(seed: int = 0) -> dict:
    key = jax.random.key(seed)
    ks = jax.random.split(key, 20)
    nrm = lambda k, shape, s: s * jax.random.normal(k, shape, F32)
    return {
        'x': nrm(ks[0], (BATCH, SEQ, D_MODEL), 1.0),
        'c': nrm(ks[1], (BATCH, D_MODEL), 1.0),
        'ctx': nrm(ks[2], (BATCH, CTX_LEN, D_MODEL), 1.0),
        'c_ctx': nrm(ks[3], (D_MODEL,), 1.0),
        'w_mod': nrm(ks[4], (DEPTH, D_MODEL, N_MOD * D_MODEL), 0.3 * D_MODEL ** -0.5),
        'b_mod': nrm(ks[5], (DEPTH, N_MOD * D_MODEL), 0.01),
        'norm_w': 1.0 + nrm(ks[6], (DEPTH, 3, D_MODEL), 0.05),
        'ffn1_w_gate': nrm(ks[7], (DEPTH, D_MODEL, D_FF), D_MODEL ** -0.5),
        'ffn1_w_up': nrm(ks[8], (DEPTH, D_MODEL, D_FF), D_MODEL ** -0.5),
        'ffn1_w_down': nrm(ks[9], (DEPTH, D_FF, D_MODEL), D_FF ** -0.5),
        'w_in': nrm(ks[10], (DEPTH, D_MODEL, IN_COLS), D_MODEL ** -0.5),
        'hg_lb_logits': nrm(ks[11], (2, DEPTH + 1, HG_WIDTH), 1.0),
        'hg_norm_w': 1.0 + nrm(ks[12], (DEPTH, HG_HEAD_DIM), 0.05),
        'na_rpb': nrm(ks[13], (DEPTH, NA_HEADS, 2 * NA_MAX_KH - 1, 2 * NA_KW - 1), 0.02),
        'w_out': nrm(ks[14], (DEPTH, MIX_WIDTH, D_MODEL), MIX_WIDTH ** -0.5),
        'ffn2_w_gate': nrm(ks[15], (DEPTH, D_MODEL, D_FF), D_MODEL ** -0.5),
        'ffn2_w_up': nrm(ks[16], (DEPTH, D_MODEL, D_FF), D_MODEL ** -0.5),
        'ffn2_w_down': nrm(ks[17], (DEPTH, D_FF, D_MODEL), D_FF ** -0.5),
        'final_norm_w': 1.0 + nrm(ks[18], (D_MODEL,), 0.05),
    }


def reference(x, c, ctx, c_ctx, w_mod, b_mod, norm_w, ffn1_w_gate, ffn1_w_up, ffn1_w_down,
              w_in, hg_lb_logits, hg_norm_w, na_rpb, w_out, ffn2_w_gate, ffn2_w_up, ffn2_w_down,
              final_norm_w):
    B, T, D = x.shape
    rows = T // GRID_W
    cos, sin = axial_rope_tables(T)
    lb_all = jnp.cumsum(jax.nn.softmax(hg_lb_logits.astype(F32), axis=1), axis=1)
    h_l, h_c = x, ctx
    for layer in range(DEPTH):
        last = layer == DEPTH - 1
        m_l = (jax.nn.silu(c) @ w_mod[layer] + b_mod[layer]).reshape(B, N_MOD, 1, D)
        m_c = (jax.nn.silu(c_ctx) @ w_mod[layer] + b_mod[layer]).reshape(1, N_MOD, 1, D)
        ffn1 = (ffn1_w_gate[layer], ffn1_w_up[layer], ffn1_w_down[layer])
        ffn2 = (ffn2_w_gate[layer], ffn2_w_up[layer], ffn2_w_down[layer])
        h_l = adaln_half_ffn(h_l, norm_w[layer, 0], m_l[:, 0], m_l[:, 1], m_l[:, 2], *ffn1)
        h_c = adaln_half_ffn(h_c, norm_w[layer, 0], m_c[:, 0], m_c[:, 1], m_c[:, 2], *ffn1)
        u_l = modulate(rmsnorm(h_l, norm_w[layer, 1]), m_l[:, 3], m_l[:, 4])
        u_c = modulate(rmsnorm(h_c, norm_w[layer, 1]), m_c[:, 3], m_c[:, 4])
        mix_l, mix_c = hybrid_mixer(u_l, u_c, w_in[layer], lb_all[:, layer], hg_norm_w[layer],
                                    na_rpb[layer], cos, sin, rows, not last)
        h_l = h_l + m_l[:, 5] * (mix_l @ w_out[layer])
        if not last:
            h_c = h_c + m_c[:, 5] * (mix_c @ w_out[layer])
            h_c = adaln_half_ffn(h_c, norm_w[layer, 2], m_c[:, 6], m_c[:, 7], m_c[:, 8], *ffn2)
        h_l = adaln_half_ffn(h_l, norm_w[layer, 2], m_l[:, 6], m_l[:, 7], m_l[:, 8], *ffn2)
    return rmsnorm(h_l, final_norm_w)
```

```python
import functools

import numpy as np
import jax
import jax.numpy as jnp
from jax import lax
from jax.experimental import pallas as pl
from jax.experimental.pallas import tpu as pltpu

F32 = jnp.float32
BF16 = jnp.bfloat16

EPS = 1e-6
GRID_W = 64
HEAD_DIM = 128
NA_KH = 8
NA_KW = 16
ROPE_BASE = 10000.0
N_MOD = 9
NEG_BIG = -1e30

V7X_VMEM_BYTES = 64 * 1024 * 1024
VMEM_LIMIT = V7X_VMEM_BYTES - 8 * 1024 * 1024

HG_CHUNK = 128
HG_LEVELS = (64, 32, 16, 8, 4)
HG_BAND = 4
NA_QROWS = 4
NA_KROWS = NA_QROWS + NA_KH - 1


def _dot(a, b):
    return jnp.dot(a, b, preferred_element_type=F32)


def _dot_nt(a, b):
    return lax.dot_general(a, b, (((1,), (1,)), ((), ())), preferred_element_type=F32)


def _dot_tn(a, b):
    return lax.dot_general(a, b, (((0,), (0,)), ((), ())), preferred_element_type=F32)


def _silu(x):
    return x * jax.nn.sigmoid(x)


def _rmsnorm(x, w):
    return x * lax.rsqrt(jnp.mean(x * x, axis=-1, keepdims=True) + EPS) * w


def _params(*semantics):
    return pltpu.CompilerParams(dimension_semantics=semantics, vmem_limit_bytes=VMEM_LIMIT)


def _mod_kernel(c_ref, w_ref, b_ref, o_ref):
    o_ref[...] = jnp.dot(_silu(c_ref[...]), w_ref[...], precision=lax.Precision.HIGHEST,
                         preferred_element_type=F32) + b_ref[...]


def _mod_call(cc, w_mod, b_mod):
    rows, d = cc.shape
    n = w_mod.shape[1]
    tn = next(t for t in (1024, 512, 256, 128) if n % t == 0)
    return pl.pallas_call(
        _mod_kernel,
        grid=(n // tn,),
        in_specs=[pl.BlockSpec((rows, d), lambda j: (0, 0)),
                  pl.BlockSpec((d, tn), lambda j: (0, j)),
                  pl.BlockSpec((1, tn), lambda j: (0, j))],
        out_specs=pl.BlockSpec((rows, tn), lambda j: (0, j)),
        out_shape=jax.ShapeDtypeStruct((rows, n), F32),
        compiler_params=_params("parallel"),
        name="mod",
    )(cc, w_mod, b_mod)


def _ffn_kernel(x_ref, mod_ref, nw_ref, wg_ref, wu_ref, wd_ref, *rest, nf, final_norm):
    if final_norm:
        fnw_ref, o_ref, u_scr, acc_scr = rest
    else:
        o_ref, u_scr, acc_scr = rest
    f = pl.program_id(1)

    @pl.when(f == 0)
    def _():
        u = _rmsnorm(x_ref[...], nw_ref[...]) * (1.0 + mod_ref[1:2, :]) + mod_ref[0:1, :]
        u_scr[...] = u.astype(BF16)
        acc_scr[...] = jnp.zeros_like(acc_scr)

    u = u_scr[...]
    a = (_silu(_dot(u, wg_ref[...])) * _dot(u, wu_ref[...])).astype(BF16)
    acc_scr[...] += _dot(a, wd_ref[...])

    @pl.when(f == nf - 1)
    def _():
        h = x_ref[...] + 0.5 * mod_ref[2:3, :] * acc_scr[...]
        if final_norm:
            h = _rmsnorm(h, fnw_ref[...])
        o_ref[...] = h


def _ffn_call(x, mod3, nw, wg, wu, wd, final_nw=None, tm=512, tf=512):
    t, d = x.shape
    dff = wg.shape[1]
    tm = min(tm, t)
    nf = dff // tf
    assert t % tm == 0 and dff % tf == 0
    final_norm = final_nw is not None
    in_specs = [pl.BlockSpec((tm, d), lambda i, f: (i, 0)),
                pl.BlockSpec((3, d), lambda i, f: (0, 0)),
                pl.BlockSpec((1, d), lambda i, f: (0, 0)),
                pl.BlockSpec((d, tf), lambda i, f: (0, f)),
                pl.BlockSpec((d, tf), lambda i, f: (0, f)),
                pl.BlockSpec((tf, d), lambda i, f: (f, 0))]
    args = [x, mod3, nw, wg, wu, wd]
    if final_norm:
        in_specs.append(pl.BlockSpec((1, d), lambda i, f: (0, 0)))
        args.append(final_nw)
    return pl.pallas_call(
        functools.partial(_ffn_kernel, nf=nf, final_norm=final_norm),
        grid=(t // tm, nf),
        in_specs=in_specs,
        out_specs=pl.BlockSpec((tm, d), lambda i, f: (i, 0)),
        out_shape=jax.ShapeDtypeStruct((t, d), F32),
        scratch_shapes=[pltpu.VMEM((tm, d), BF16), pltpu.VMEM((tm, d), F32)],
        compiler_params=_params("parallel", "arbitrary"),
        name="ffn",
    )(*args)


def _inproj_kernel(x_ref, mod_ref, nw_ref, w_ref, o_ref, u_scr):
    @pl.when(pl.program_id(1) == 0)
    def _():
        u = _rmsnorm(x_ref[...], nw_ref[...]) * (1.0 + mod_ref[1:2, :]) + mod_ref[0:1, :]
        u_scr[...] = u.astype(BF16)

    o_ref[...] = _dot(u_scr[...], w_ref[...])


def _inproj_call(x, mod2, nw, w, tm=1024, tn=1024):
    t, d = x.shape
    n = w.shape[1]
    tm = min(tm, t)
    assert t % tm == 0 and n % tn == 0
    return pl.pallas_call(
        _inproj_kernel,
        grid=(t // tm, n // tn),
        in_specs=[pl.BlockSpec((tm, d), lambda i, j: (i, 0)),
                  pl.BlockSpec((2, d), lambda i, j: (0, 0)),
                  pl.BlockSpec((1, d), lambda i, j: (0, 0)),
                  pl.BlockSpec((d, tn), lambda i, j: (0, j))],
        out_specs=pl.BlockSpec((tm, tn), lambda i, j: (i, j)),
        out_shape=jax.ShapeDtypeStruct((t, n), F32),
        scratch_shapes=[pltpu.VMEM((tm, d), BF16)],
        compiler_params=_params("parallel", "arbitrary"),
        name="inproj",
    )(x, mod2, nw, w)


def _split3(x):
    hi = x.astype(BF16)
    r = x - hi.astype(F32)
    mid = r.astype(BF16)
    lo = (r - mid.astype(F32)).astype(BF16)
    return hi, mid, lo


def _hg_chunk(q_raw, f_raw, v, lb, tri, lvl, b_scr, st_scr, emit, is_lat, *, rev):
    c = HG_CHUNK
    q = _silu(q_raw)
    f = lb + (1.0 - lb) * jax.nn.sigmoid(f_raw)
    k = 1.0 - f
    hi, mid, lo = _split3(jnp.log(f))
    b = _dot(tri, hi) + _dot(tri, mid) + _dot(tri, lo)
    b_scr[...] = b
    b_end = b_scr[pl.ds(0 if rev else c - 1, 1), :]
    vb = v.astype(BF16)

    @pl.when(is_lat)
    def _():
        o = _dot_nt((q * jnp.exp(b)).astype(BF16), st_scr[...].astype(BF16))
        pos = lax.broadcasted_iota(jnp.int32, (c, HEAD_DIM), 0)
        s_off = jnp.zeros((c, c), F32)
        for m in HG_LEVELS:
            rows = [b_scr[pl.ds(a * 2 * m + (m if rev else m - 1), 1), :] for a in range(c // (2 * m))]
            rows = [jnp.broadcast_to(r, (2 * m, HEAD_DIM)) for r in rows]
            bmid = rows[0] if len(rows) == 1 else jnp.concatenate(rows, axis=0)
            half = pos & (2 * m - 1)
            q_side = (half < m) if rev else (half >= m)
            w = jnp.exp(jnp.where(q_side, b - bmid, bmid - b))
            qm = jnp.where(q_side, q * w, 0.0).astype(BF16)
            km = jnp.where(q_side, 0.0, k * w).astype(BF16)
            s_off = s_off + jnp.where(lvl == m, _dot_nt(qm, km), 0.0)
        o = o + _dot(s_off.astype(BF16), vb)
        o = o + jnp.sum(q * k, axis=-1, keepdims=True) * v
        inblk = pos & (HG_BAND - 1)
        prod = f
        for d in range(1, HG_BAND):
            shift = (c - d) if rev else d
            if d > 1:
                prod = prod * pltpu.roll(f, (c - d + 1) if rev else d - 1, 0)
            coef = jnp.sum(q * pltpu.roll(k, shift, 0) * prod, axis=-1, keepdims=True)
            valid = (inblk < HG_BAND - d) if rev else (inblk >= d)
            o = o + jnp.where(valid, coef * pltpu.roll(v, shift, 0), 0.0)
        emit(o)

    kh = (k * jnp.exp(b_end - b)).astype(BF16)
    st_scr[...] = st_scr[...] * jnp.exp(b_end) + _dot_tn(vb, kh)


def _hg_kernel(qc_ref, fc_ref, vc_ref, ql_ref, fl_ref, vl_ref, lb_ref, tri_ref, lvl_ref, *rest,
               rev, n_ctx, readout):
    if readout:
        of_ref, g_ref, nw_ref, o_ref, b_scr, st_scr = rest
    else:
        o_ref, b_scr, st_scr = rest
    j = pl.program_id(1)

    @pl.when(j == 0)
    def _():
        st_scr[...] = jnp.zeros_like(st_scr)

    is_lat = j >= n_ctx
    n_chunks = ql_ref.shape[0] // HG_CHUNK
    order = range(n_chunks - 1, -1, -1) if rev else range(n_chunks)
    for ci in order:
        sl = pl.ds(ci * HG_CHUNK, HG_CHUNK)

        def emit(o, sl=sl):
            if readout:
                tot = of_ref[sl, :] + o
                o_ref[sl, :] = (_rmsnorm(tot, nw_ref[...]) * _silu(g_ref[sl, :])).astype(o_ref.dtype)
            else:
                o_ref[sl, :] = o

        pick = lambda c_ref, l_ref, sl=sl: jnp.where(is_lat, l_ref[sl, :], c_ref[sl, :])
        _hg_chunk(pick(qc_ref, ql_ref), pick(fc_ref, fl_ref), pick(vc_ref, vl_ref), lb_ref[...],
                  tri_ref[...], lvl_ref[...], b_scr, st_scr, emit, is_lat, rev=rev)


def _hg_level_table(rev):
    t = np.arange(HG_CHUNK)[:, None]
    s = np.arange(HG_CHUNK)[None, :]
    x = t ^ s
    msb = np.where(x > 0, 2 ** np.floor(np.log2(np.maximum(x, 1))).astype(np.int64), 0)
    causal = (s > t) if rev else (s < t)
    return np.where(causal & (msb >= HG_BAND), msb, 0).astype(np.int32)


def _hg_call(p_ctx, p_lat, lb_dir, o_fw, norm_w, *, rev, n_heads, tb=256):
    t_ctx, t_lat = p_ctx.shape[0], p_lat.shape[0]
    assert t_ctx % tb == 0 and t_lat % tb == 0 and tb % HG_CHUNK == 0
    nc, nl = t_ctx // tb, t_lat // tb
    readout = o_fw is not None
    f_col = 2 * n_heads if rev else n_heads

    def ctx_row(j):
        i = jnp.minimum(j, nc - 1)
        return nc - 1 - i if rev else i

    def lat_row(j):
        i = jnp.maximum(j - nc, 0)
        return nl - 1 - i if rev else i

    def ctx_spec(col):
        return pl.BlockSpec((tb, HEAD_DIM), lambda h, j: (ctx_row(j), col + h))

    def lat_spec(col):
        return pl.BlockSpec((tb, HEAD_DIM), lambda h, j: (lat_row(j), col + h))

    tri = np.tril(np.ones((HG_CHUNK, HG_CHUNK), np.float32))
    tri = jnp.asarray(tri.T if rev else tri, BF16)
    lvl = jnp.asarray(_hg_level_table(rev))
    const_spec = pl.BlockSpec((HG_CHUNK, HG_CHUNK), lambda h, j: (0, 0))
    in_specs = [ctx_spec(0), ctx_spec(f_col), ctx_spec(3 * n_heads),
                lat_spec(0), lat_spec(f_col), lat_spec(3 * n_heads),
                pl.BlockSpec((1, HEAD_DIM), lambda h, j: (0, h)), const_spec, const_spec]
    args = [p_ctx, p_ctx, p_ctx, p_lat, p_lat, p_lat, lb_dir, tri, lvl]
    out_spec = pl.BlockSpec((tb, HEAD_DIM), lambda h, j: (lat_row(j), h))
    if readout:
        in_specs += [out_spec, lat_spec(4 * n_heads), pl.BlockSpec((1, HEAD_DIM), lambda h, j: (0, 0))]
        args += [o_fw, p_lat, norm_w]
    return pl.pallas_call(
        functools.partial(_hg_kernel, rev=rev, n_ctx=nc, readout=readout),
        grid=(n_heads, nc + nl),
        in_specs=in_specs,
        out_specs=out_spec,
        out_shape=jax.ShapeDtypeStruct((t_lat, n_heads * HEAD_DIM), BF16 if readout else F32),
        scratch_shapes=[pltpu.VMEM((HG_CHUNK, HEAD_DIM), F32), pltpu.VMEM((HEAD_DIM, HEAD_DIM), F32)],
        compiler_params=_params("parallel", "arbitrary"),
        name="hgrn_bwd" if rev else "hgrn_fwd",
    )(*args)


def _rope(x, cos, sin_signed, low):
    return x * cos + jnp.where(low, pltpu.roll(x, 96, 1), pltpu.roll(x, 32, 1)) * sin_signed


def _na_kernel(q_ref, k_ref, v_ref, kc_ref, vc_ref, cosq_ref, sinq_ref, cosk_ref, sink_ref, bias_ref,
               o_ref, k_scr, v_scr, *, rows, prep_rows):
    rb = pl.program_id(1)
    n_rb = pl.num_programs(1)
    t = k_ref.shape[0]

    def low_mask(n):
        lane = lax.broadcasted_iota(jnp.int32, (n, HEAD_DIM), 1)
        return (lane & (HEAD_DIM // 2 - 1)) < HEAD_DIM // 4

    @pl.when(rb == 0)
    def _():
        low = low_mask(prep_rows)

        def body(i, carry):
            sl = pl.ds(pl.multiple_of(i * prep_rows, prep_rows), prep_rows)
            k_scr[sl, :] = _rope(k_ref[sl, :], cosk_ref[sl, :], sink_ref[sl, :], low).astype(BF16)
            v_scr[sl, :] = v_ref[sl, :].astype(BF16)
            return carry

        lax.fori_loop(0, t // prep_rows, body, 0)

    nq = q_ref.shape[0]
    q = _rope(q_ref[...], cosq_ref[...], sinq_ref[...], low_mask(nq)) * (HEAD_DIM ** -0.5)
    q = q.astype(BF16)
    key_row0 = jnp.clip(rb * NA_QROWS - NA_KH // 2, 0, rows - NA_KROWS)
    win = pl.ds(pl.multiple_of(key_row0 * GRID_W, GRID_W), NA_KROWS * GRID_W)
    kw = k_scr[win, :]
    vw = v_scr[win, :]
    edge = jnp.where(rb == 0, 0, jnp.where(rb == n_rb - 1, 2, 1))
    s_w = _dot_nt(q, kw) + bias_ref[edge]
    s_c = _dot_nt(q, kc_ref[...].astype(BF16))
    m = jnp.maximum(jnp.max(s_w, axis=-1, keepdims=True), jnp.max(s_c, axis=-1, keepdims=True))
    p_w = jnp.exp(s_w - m)
    p_c = jnp.exp(s_c - m)
    denom = jnp.sum(p_w, axis=-1, keepdims=True) + jnp.sum(p_c, axis=-1, keepdims=True)
    o = _dot(p_w.astype(BF16), vw) + _dot(p_c.astype(BF16), vc_ref[...].astype(BF16))
    o_ref[...] = (o / denom).astype(o_ref.dtype)


def _na_bias(rpb, rows):
    a = np.arange(NA_QROWS)[:, None]
    b = np.arange(NA_KROWS)[None, :]
    row_in = np.stack([np.broadcast_to(b < NA_KH, (NA_QROWS, NA_KROWS)),
                       (b - a >= 0) & (b - a < NA_KH),
                       np.broadcast_to(b >= NA_KROWS - NA_KH, (NA_QROWS, NA_KROWS))])
    dr = np.stack([b - a + NA_KH - 1, b - a + NA_KH // 2 - 1, b - a + NA_KH - NA_KROWS + NA_QROWS - 1])
    dr = np.clip(dr, 0, 2 * NA_KH - 2)
    col = np.arange(GRID_W)
    col_start = np.clip(col - NA_KW // 2, 0, GRID_W - NA_KW)
    col_in = (col[None, :] >= col_start[:, None]) & (col[None, :] < col_start[:, None] + NA_KW)
    dc = np.clip(col[None, :] - col[:, None], 1 - NA_KW, NA_KW - 1) + (NA_KW - 1)
    vals = rpb.astype(F32)[:, dr[:, :, None, :, None], dc[None, None, :, None, :]]
    ok = row_in[:, :, None, :, None] & col_in[None, None, :, None, :]
    bias = jnp.where(ok[None], vals, NEG_BIG)
    return bias.reshape(rpb.shape[0], 3, NA_QROWS * GRID_W, NA_KROWS * GRID_W)


def _rope_tables(n_tokens):
    t = jnp.arange(n_tokens)
    row = (t // GRID_W).astype(F32)
    col = (t % GRID_W).astype(F32)
    quarter = HEAD_DIM // 4
    inv = ROPE_BASE ** (-jnp.arange(quarter, dtype=F32) / quarter)
    ang_r = row[:, None] * inv[None, :]
    ang_c = col[:, None] * inv[None, :]
    ang = jnp.concatenate([ang_r, ang_r, ang_c, ang_c], axis=-1)
    sign = np.where((np.arange(HEAD_DIM) % (HEAD_DIM // 2)) < quarter, -1.0, 1.0).astype(np.float32)
    return jnp.cos(ang), jnp.sin(ang) * sign


def _na_call(p_ctx, p_lat, rpb, *, col0, n_heads):
    t_ctx, t = p_ctx.shape[0], p_lat.shape[0]
    rows = t // GRID_W
    assert t % GRID_W == 0 and rows % NA_QROWS == 0 and rows >= NA_KROWS + NA_QROWS
    nq = NA_QROWS * GRID_W
    prep_rows = 512
    assert t % prep_rows == 0
    cos, sin_signed = _rope_tables(t)
    bias = _na_bias(rpb, rows)
    full = lambda col: pl.BlockSpec((t, HEAD_DIM), lambda h, r: (0, col + h))
    ctx = lambda col: pl.BlockSpec((t_ctx, HEAD_DIM), lambda h, r: (0, col + h))
    tab_q = pl.BlockSpec((nq, HEAD_DIM), lambda h, r: (r, 0))
    tab_k = pl.BlockSpec((t, HEAD_DIM), lambda h, r: (0, 0))
    return pl.pallas_call(
        functools.partial(_na_kernel, rows=rows, prep_rows=prep_rows),
        grid=(n_heads, rows // NA_QROWS),
        in_specs=[pl.BlockSpec((nq, HEAD_DIM), lambda h, r: (r, col0 + h)),
                  full(col0 + n_heads), full(col0 + 2 * n_heads),
                  ctx(col0 + n_heads), ctx(col0 + 2 * n_heads),
                  tab_q, tab_q, tab_k, tab_k,
                  pl.BlockSpec((None, 3, nq, NA_KROWS * GRID_W), lambda h, r: (h, 0, 0, 0))],
        out_specs=pl.BlockSpec((nq, HEAD_DIM), lambda h, r: (r, h)),
        out_shape=jax.ShapeDtypeStruct((t, n_heads * HEAD_DIM), BF16),
        scratch_shapes=[pltpu.VMEM((t, HEAD_DIM), BF16), pltpu.VMEM((t, HEAD_DIM), BF16)],
        compiler_params=_params("parallel", "arbitrary"),
        name="natten",
    )(p_lat, p_lat, p_lat, p_ctx, p_ctx, cos, sin_signed, cos, sin_signed, bias)


def _outproj_kernel(hg_ref, na_ref, h_ref, gate_ref, w_ref, o_ref):
    hw = hg_ref.shape[1]
    mix = _dot(hg_ref[...], w_ref[:hw, :]) + _dot(na_ref[...], w_ref[hw:, :])
    o_ref[...] = h_ref[...] + gate_ref[...] * mix


def _outproj_call(hg, na, h, gate, w, tm=512):
    t, d = h.shape
    hw, nw = hg.shape[1], na.shape[1]
    assert t % tm == 0
    return pl.pallas_call(
        _outproj_kernel,
        grid=(t // tm,),
        in_specs=[pl.BlockSpec((tm, hw), lambda i: (i, 0)),
                  pl.BlockSpec((tm, nw), lambda i: (i, 0)),
                  pl.BlockSpec((tm, d), lambda i: (i, 0)),
                  pl.BlockSpec((1, d), lambda i: (0, 0)),
                  pl.BlockSpec((hw + nw, d), lambda i: (0, 0))],
        out_specs=pl.BlockSpec((tm, d), lambda i: (i, 0)),
        out_shape=jax.ShapeDtypeStruct((t, d), F32),
        compiler_params=_params("parallel"),
        name="outproj",
    )(hg, na, h, gate, w)


def kernel(x, c, ctx, c_ctx, w_mod, b_mod, norm_w, ffn1_w_gate, ffn1_w_up, ffn1_w_down, w_in, hg_lb_logits,
           hg_norm_w, na_rpb, w_out, ffn2_w_gate, ffn2_w_up, ffn2_w_down, final_norm_w):
    batch, _, d = x.shape
    depth = w_mod.shape[0]
    assert batch == 1 and depth == 1
    hg_heads = hg_lb_logits.shape[-1] // HEAD_DIM
    na_heads = na_rpb.shape[1]
    bf = lambda w: w.astype(BF16)
    row = lambda v: v.reshape(1, -1)

    cc = jnp.zeros((8, d), F32).at[0].set(c[0]).at[1].set(c_ctx)
    mods = _mod_call(cc, w_mod[0], row(b_mod[0]))
    m_l = mods[0].reshape(N_MOD, d)
    m_c = mods[1].reshape(N_MOD, d)
    lb = jnp.cumsum(jax.nn.softmax(hg_lb_logits.astype(F32), axis=1), axis=1)[:, 0]

    ffn1 = (bf(ffn1_w_gate[0]), bf(ffn1_w_up[0]), bf(ffn1_w_down[0]))
    ffn2 = (bf(ffn2_w_gate[0]), bf(ffn2_w_up[0]), bf(ffn2_w_down[0]))
    w_in_b, w_out_b = bf(w_in[0]), bf(w_out[0])

    h_l = _ffn_call(x[0], m_l[0:3], row(norm_w[0, 0]), *ffn1)
    h_c = _ffn_call(ctx[0], m_c[0:3], row(norm_w[0, 0]), *ffn1)
    p_l = _inproj_call(h_l, m_l[3:5], row(norm_w[0, 1]), w_in_b)
    p_c = _inproj_call(h_c, m_c[3:5], row(norm_w[0, 1]), w_in_b)

    o_fw = _hg_call(p_c, p_l, lb[0:1], None, None, rev=False, n_heads=hg_heads)
    hg = _hg_call(p_c, p_l, lb[1:2], o_fw, row(hg_norm_w[0]), rev=True, n_heads=hg_heads)
    na = _na_call(p_c, p_l, na_rpb[0], col0=5 * hg_heads, n_heads=na_heads)

    h_l = _outproj_call(hg, na, h_l, m_l[5:6], w_out_b)
    out = _ffn_call(h_l, m_l[6:9], row(norm_w[0, 2]), *ffn2, final_nw=row(final_norm_w))
    return out[None]
```

```python
import functools

import numpy as np
import jax
import jax.numpy as jnp
from jax import lax
from jax.experimental import pallas as pl
from jax.experimental.pallas import tpu as pltpu

F32 = jnp.float32
BF16 = jnp.bfloat16

EPS = 1e-6
GRID_W = 64
HEAD_DIM = 128
NA_KH = 8
NA_KW = 16
ROPE_BASE = 10000.0
N_MOD = 9
NEG_BIG = -1e30

V7X_VMEM_BYTES = 64 * 1024 * 1024
VMEM_LIMIT = V7X_VMEM_BYTES - 8 * 1024 * 1024

HG_CHUNK = 128
HG_LEVELS = (64, 32, 16, 8, 4)
HG_BAND = 4
NA_QROWS = 4
NA_KROWS = NA_QROWS + NA_KH - 1


def _dot(a, b):
    return jnp.dot(a, b, preferred_element_type=F32)


def _dot_nt(a, b):
    return lax.dot_general(a, b, (((1,), (1,)), ((), ())), preferred_element_type=F32)


def _dot_tn(a, b):
    return lax.dot_general(a, b, (((0,), (0,)), ((), ())), preferred_element_type=F32)


def _silu(x):
    return x * jax.nn.sigmoid(x)


def _rmsnorm(x, w):
    return x * lax.rsqrt(jnp.mean(x * x, axis=-1, keepdims=True) + EPS) * w


def _params(*semantics):
    return pltpu.CompilerParams(dimension_semantics=semantics, vmem_limit_bytes=VMEM_LIMIT)


def _mod_kernel(c_ref, w_ref, b_ref, o_ref):
    d, tn = w_ref.shape
    sub = 8

    def body(i, accs):
        rows = pl.ds(pl.multiple_of(i * sub, sub), sub)
        w = w_ref[rows, :]
        sc = _silu(c_ref[rows, :])
        return tuple(acc + w * sc[:, r:r + 1] for r, acc in enumerate(accs))

    zero = jnp.zeros((sub, tn), F32)
    accs = lax.fori_loop(0, d // sub, body, (zero,) * c_ref.shape[1], unroll=8)
    for r, acc in enumerate(accs):
        o_ref[r:r + 1, :] = jnp.sum(acc, axis=0, keepdims=True) + b_ref[...]


def _mod_call(cc, w_mod, b_mod):
    d, n_vec = cc.shape
    n = w_mod.shape[1]
    tn = next(t for t in (1024, 512, 256, 128) if n % t == 0)
    return pl.pallas_call(
        _mod_kernel,
        grid=(n // tn,),
        in_specs=[pl.BlockSpec((d, n_vec), lambda j: (0, 0)),
                  pl.BlockSpec((d, tn), lambda j: (0, j)),
                  pl.BlockSpec((1, tn), lambda j: (0, j))],
        out_specs=pl.BlockSpec((n_vec, tn), lambda j: (0, j)),
        out_shape=jax.ShapeDtypeStruct((n_vec, n), F32),
        compiler_params=_params("parallel"),
        name="mod",
    )(cc, w_mod, b_mod)


def _ffn_kernel(x_ref, mod_ref, nw_ref, wg_ref, wu_ref, wd_ref, *rest, nf, final_norm):
    if final_norm:
        fnw_ref, o_ref, u_scr, acc_scr = rest
    else:
        o_ref, u_scr, acc_scr = rest
    f = pl.program_id(1)

    @pl.when(f == 0)
    def _():
        u = _rmsnorm(x_ref[...], nw_ref[...]) * (1.0 + mod_ref[1:2, :]) + mod_ref[0:1, :]
        u_scr[...] = u.astype(BF16)
        acc_scr[...] = jnp.zeros_like(acc_scr)

    u = u_scr[...]
    a = (_silu(_dot(u, wg_ref[...])) * _dot(u, wu_ref[...])).astype(BF16)
    acc_scr[...] += _dot(a, wd_ref[...])

    @pl.when(f == nf - 1)
    def _():
        h = x_ref[...] + 0.5 * mod_ref[2:3, :] * acc_scr[...]
        if final_norm:
            h = _rmsnorm(h, fnw_ref[...])
        o_ref[...] = h


def _ffn_call(x, mod3, nw, wg, wu, wd, final_nw=None, tm=512, tf=512):
    t, d = x.shape
    dff = wg.shape[1]
    tm = min(tm, t)
    nf = dff // tf
    assert t % tm == 0 and dff % tf == 0
    final_norm = final_nw is not None
    in_specs = [pl.BlockSpec((tm, d), lambda i, f: (i, 0)),
                pl.BlockSpec((3, d), lambda i, f: (0, 0)),
                pl.BlockSpec((1, d), lambda i, f: (0, 0)),
                pl.BlockSpec((d, tf), lambda i, f: (0, f)),
                pl.BlockSpec((d, tf), lambda i, f: (0, f)),
                pl.BlockSpec((tf, d), lambda i, f: (f, 0))]
    args = [x, mod3, nw, wg, wu, wd]
    if final_norm:
        in_specs.append(pl.BlockSpec((1, d), lambda i, f: (0, 0)))
        args.append(final_nw)
    return pl.pallas_call(
        functools.partial(_ffn_kernel, nf=nf, final_norm=final_norm),
        grid=(t // tm, nf),
        in_specs=in_specs,
        out_specs=pl.BlockSpec((tm, d), lambda i, f: (i, 0)),
        out_shape=jax.ShapeDtypeStruct((t, d), F32),
        scratch_shapes=[pltpu.VMEM((tm, d), BF16), pltpu.VMEM((tm, d), F32)],
        compiler_params=_params("parallel", "arbitrary"),
        name="ffn",
    )(*args)


def _inproj_kernel(x_ref, mod_ref, nw_ref, w_ref, o_ref, u_scr):
    @pl.when(pl.program_id(1) == 0)
    def _():
        u = _rmsnorm(x_ref[...], nw_ref[...]) * (1.0 + mod_ref[1:2, :]) + mod_ref[0:1, :]
        u_scr[...] = u.astype(BF16)

    o_ref[...] = _dot(u_scr[...], w_ref[...])


def _inproj_call(x, mod2, nw, w, tm=1024, tn=1024):
    t, d = x.shape
    n = w.shape[1]
    tm = min(tm, t)
    assert t % tm == 0 and n % tn == 0
    return pl.pallas_call(
        _inproj_kernel,
        grid=(t // tm, n // tn),
        in_specs=[pl.BlockSpec((tm, d), lambda i, j: (i, 0)),
                  pl.BlockSpec((2, d), lambda i, j: (0, 0)),
                  pl.BlockSpec((1, d), lambda i, j: (0, 0)),
                  pl.BlockSpec((d, tn), lambda i, j: (0, j))],
        out_specs=pl.BlockSpec((tm, tn), lambda i, j: (i, j)),
        out_shape=jax.ShapeDtypeStruct((t, n), F32),
        scratch_shapes=[pltpu.VMEM((tm, d), BF16)],
        compiler_params=_params("parallel", "arbitrary"),
        name="inproj",
    )(x, mod2, nw, w)


def _split3(x):
    hi = x.astype(BF16)
    r = x - hi.astype(F32)
    mid = r.astype(BF16)
    lo = (r - mid.astype(F32)).astype(BF16)
    return hi, mid, lo


def _hg_chunk(q_raw, f_raw, v, lb, tri, lvl, b_scr, st_scr, emit, is_lat, *, rev):
    c = HG_CHUNK
    q = _silu(q_raw)
    f = lb + (1.0 - lb) * jax.nn.sigmoid(f_raw)
    k = 1.0 - f
    hi, mid, lo = _split3(jnp.log(f))
    b = _dot(tri, hi) + _dot(tri, mid) + _dot(tri, lo)
    b_scr[...] = b
    b_end = b_scr[pl.ds(0 if rev else c - 1, 1), :]
    vb = v.astype(BF16)

    @pl.when(is_lat)
    def _():
        o = _dot_nt((q * jnp.exp(b)).astype(BF16), st_scr[...].astype(BF16))
        pos = lax.broadcasted_iota(jnp.int32, (c, HEAD_DIM), 0)
        s_off = jnp.zeros((c, c), F32)
        for m in HG_LEVELS:
            rows = [b_scr[pl.ds(a * 2 * m + (m if rev else m - 1), 1), :] for a in range(c // (2 * m))]
            rows = [jnp.broadcast_to(r, (2 * m, HEAD_DIM)) for r in rows]
            bmid = rows[0] if len(rows) == 1 else jnp.concatenate(rows, axis=0)
            half = pos & (2 * m - 1)
            q_side = (half < m) if rev else (half >= m)
            w = jnp.exp(jnp.where(q_side, b - bmid, bmid - b))
            qm = jnp.where(q_side, q * w, 0.0).astype(BF16)
            km = jnp.where(q_side, 0.0, k * w).astype(BF16)
            s_off = s_off + jnp.where(lvl == m, _dot_nt(qm, km), 0.0)
        o = o + _dot(s_off.astype(BF16), vb)
        o = o + jnp.sum(q * k, axis=-1, keepdims=True) * v
        inblk = pos & (HG_BAND - 1)
        prod = f
        for d in range(1, HG_BAND):
            shift = (c - d) if rev else d
            if d > 1:
                prod = prod * pltpu.roll(f, (c - d + 1) if rev else d - 1, 0)
            coef = jnp.sum(q * pltpu.roll(k, shift, 0) * prod, axis=-1, keepdims=True)
            valid = (inblk < HG_BAND - d) if rev else (inblk >= d)
            o = o + jnp.where(valid, coef * pltpu.roll(v, shift, 0), 0.0)
        emit(o)

    kh = (k * jnp.exp(b_end - b)).astype(BF16)
    st_scr[...] = st_scr[...] * jnp.exp(b_end) + _dot_tn(vb, kh)


def _hg_kernel(qc_ref, fc_ref, vc_ref, ql_ref, fl_ref, vl_ref, lb_ref, tri_ref, lvl_ref, *rest,
               rev, n_ctx, readout):
    if readout:
        of_ref, g_ref, nw_ref, o_ref, b_scr, st_scr = rest
    else:
        o_ref, b_scr, st_scr = rest
    j = pl.program_id(1)

    @pl.when(j == 0)
    def _():
        st_scr[...] = jnp.zeros_like(st_scr)

    is_lat = j >= n_ctx
    n_chunks = ql_ref.shape[0] // HG_CHUNK
    order = range(n_chunks - 1, -1, -1) if rev else range(n_chunks)
    for ci in order:
        sl = pl.ds(ci * HG_CHUNK, HG_CHUNK)

        def emit(o, sl=sl):
            if readout:
                tot = of_ref[sl, :] + o
                o_ref[sl, :] = (_rmsnorm(tot, nw_ref[...]) * _silu(g_ref[sl, :])).astype(o_ref.dtype)
            else:
                o_ref[sl, :] = o

        pick = lambda c_ref, l_ref, sl=sl: jnp.where(is_lat, l_ref[sl, :], c_ref[sl, :])
        _hg_chunk(pick(qc_ref, ql_ref), pick(fc_ref, fl_ref), pick(vc_ref, vl_ref), lb_ref[...],
                  tri_ref[...], lvl_ref[...], b_scr, st_scr, emit, is_lat, rev=rev)


def _hg_level_table(rev):
    t = np.arange(HG_CHUNK)[:, None]
    s = np.arange(HG_CHUNK)[None, :]
    x = t ^ s
    msb = np.where(x > 0, 2 ** np.floor(np.log2(np.maximum(x, 1))).astype(np.int64), 0)
    causal = (s > t) if rev else (s < t)
    return np.where(causal & (msb >= HG_BAND), msb, 0).astype(np.int32)


def _hg_call(p_ctx, p_lat, lb_dir, o_fw, norm_w, *, rev, n_heads, tb=256):
    t_ctx, t_lat = p_ctx.shape[0], p_lat.shape[0]
    assert t_ctx % tb == 0 and t_lat % tb == 0 and tb % HG_CHUNK == 0
    nc, nl = t_ctx // tb, t_lat // tb
    readout = o_fw is not None
    f_col = 2 * n_heads if rev else n_heads

    def ctx_row(j):
        i = jnp.minimum(j, nc - 1)
        return nc - 1 - i if rev else i

    def lat_row(j):
        i = jnp.maximum(j - nc, 0)
        return nl - 1 - i if rev else i

    def ctx_spec(col):
        return pl.BlockSpec((tb, HEAD_DIM), lambda h, j: (ctx_row(j), col + h))

    def lat_spec(col):
        return pl.BlockSpec((tb, HEAD_DIM), lambda h, j: (lat_row(j), col + h))

    tri = np.tril(np.ones((HG_CHUNK, HG_CHUNK), np.float32))
    tri = jnp.asarray(tri.T if rev else tri, BF16)
    lvl = jnp.asarray(_hg_level_table(rev))
    const_spec = pl.BlockSpec((HG_CHUNK, HG_CHUNK), lambda h, j: (0, 0))
    in_specs = [ctx_spec(0), ctx_spec(f_col), ctx_spec(3 * n_heads),
                lat_spec(0), lat_spec(f_col), lat_spec(3 * n_heads),
                pl.BlockSpec((1, HEAD_DIM), lambda h, j: (0, h)), const_spec, const_spec]
    args = [p_ctx, p_ctx, p_ctx, p_lat, p_lat, p_lat, lb_dir, tri, lvl]
    out_spec = pl.BlockSpec((tb, HEAD_DIM), lambda h, j: (lat_row(j), h))
    if readout:
        in_specs += [out_spec, lat_spec(4 * n_heads), pl.BlockSpec((1, HEAD_DIM), lambda h, j: (0, 0))]
        args += [o_fw, p_lat, norm_w]
    return pl.pallas_call(
        functools.partial(_hg_kernel, rev=rev, n_ctx=nc, readout=readout),
        grid=(n_heads, nc + nl),
        in_specs=in_specs,
        out_specs=out_spec,
        out_shape=jax.ShapeDtypeStruct((t_lat, n_heads * HEAD_DIM), BF16 if readout else F32),
        scratch_shapes=[pltpu.VMEM((HG_CHUNK, HEAD_DIM), F32), pltpu.VMEM((HEAD_DIM, HEAD_DIM), F32)],
        compiler_params=_params("parallel", "arbitrary"),
        name="hgrn_bwd" if rev else "hgrn_fwd",
    )(*args)


def _rope(x, cos, sin_signed, low):
    return x * cos + jnp.where(low, pltpu.roll(x, 96, 1), pltpu.roll(x, 32, 1)) * sin_signed


def _na_kernel(q_ref, k_ref, v_ref, kc_ref, vc_ref, cosq_ref, sinq_ref, cosk_ref, sink_ref, cbias_ref,
               o_ref, k_scr, v_scr, bias_scr, *, rows, prep_rows):
    rb = pl.program_id(1)
    n_rb = pl.num_programs(1)
    t = k_ref.shape[0]

    def low_mask(n):
        lane = lax.broadcasted_iota(jnp.int32, (n, HEAD_DIM), 1)
        return (lane & (HEAD_DIM // 2 - 1)) < HEAD_DIM // 4

    @pl.when(rb == 0)
    def _():
        low = low_mask(prep_rows)

        def body(i, carry):
            sl = pl.ds(pl.multiple_of(i * prep_rows, prep_rows), prep_rows)
            k_scr[sl, :] = _rope(k_ref[sl, :], cosk_ref[sl, :], sink_ref[sl, :], low).astype(BF16)
            v_scr[sl, :] = v_ref[sl, :].astype(BF16)
            return carry

        lax.fori_loop(0, t // prep_rows, body, 0)

        bias_scr[...] = jnp.full(bias_scr.shape, NEG_BIG, F32)
        for e in range(3):
            for a in range(NA_QROWS):
                b0 = (0, a, NA_KROWS - NA_KH)[e]
                d0 = (NA_KH - 1 - a, NA_KH // 2 - 1, NA_QROWS - 1 - a)[e]
                for i in range(NA_KH):
                    bias_scr[e, a * GRID_W:(a + 1) * GRID_W, (b0 + i) * GRID_W:(b0 + i + 1) * GRID_W] = (
                        cbias_ref[d0 + i])

    nq = q_ref.shape[0]
    q = _rope(q_ref[...], cosq_ref[...], sinq_ref[...], low_mask(nq)) * (HEAD_DIM ** -0.5)
    q = q.astype(BF16)
    key_row0 = jnp.clip(rb * NA_QROWS - NA_KH // 2, 0, rows - NA_KROWS)
    win = pl.ds(pl.multiple_of(key_row0 * GRID_W, GRID_W), NA_KROWS * GRID_W)
    kw = k_scr[win, :]
    vw = v_scr[win, :]
    edge = jnp.where(rb == 0, 0, jnp.where(rb == n_rb - 1, 2, 1))
    s_w = _dot_nt(q, kw) + bias_scr[edge]
    s_c = _dot_nt(q, kc_ref[...].astype(BF16))
    m = jnp.maximum(jnp.max(s_w, axis=-1, keepdims=True), jnp.max(s_c, axis=-1, keepdims=True))
    p_w = jnp.exp(s_w - m)
    p_c = jnp.exp(s_c - m)
    denom = jnp.sum(p_w, axis=-1, keepdims=True) + jnp.sum(p_c, axis=-1, keepdims=True)
    o = _dot(p_w.astype(BF16), vw) + _dot(p_c.astype(BF16), vc_ref[...].astype(BF16))
    o_ref[...] = (o / denom).astype(o_ref.dtype)


def _na_col_bias(rpb):
    col = np.arange(GRID_W)
    col_start = np.clip(col - NA_KW // 2, 0, GRID_W - NA_KW)
    col_in = (col[None, :] >= col_start[:, None]) & (col[None, :] < col_start[:, None] + NA_KW)
    dc = np.clip(col[None, :] - col[:, None], 1 - NA_KW, NA_KW - 1) + (NA_KW - 1)
    onehot = (dc[None] == np.arange(2 * NA_KW - 1)[:, None, None]).astype(np.float32)
    vals = jnp.einsum("hrk,kqc->hrqc", rpb.astype(F32), onehot, precision=lax.Precision.HIGHEST)
    return jnp.where(col_in[None, None], vals, NEG_BIG)


def _rope_tables(n_tokens):
    t = jnp.arange(n_tokens)
    row = (t // GRID_W).astype(F32)
    col = (t % GRID_W).astype(F32)
    quarter = HEAD_DIM // 4
    inv = ROPE_BASE ** (-jnp.arange(quarter, dtype=F32) / quarter)
    ang_r = row[:, None] * inv[None, :]
    ang_c = col[:, None] * inv[None, :]
    ang = jnp.concatenate([ang_r, ang_r, ang_c, ang_c], axis=-1)
    sign = np.where((np.arange(HEAD_DIM) % (HEAD_DIM // 2)) < quarter, -1.0, 1.0).astype(np.float32)
    return jnp.cos(ang), jnp.sin(ang) * sign


def _na_call(p_ctx, p_lat, rpb, *, col0, n_heads):
    t_ctx, t = p_ctx.shape[0], p_lat.shape[0]
    rows = t // GRID_W
    assert t % GRID_W == 0 and rows % NA_QROWS == 0 and rows >= NA_KROWS + NA_QROWS
    nq = NA_QROWS * GRID_W
    prep_rows = 512
    assert t % prep_rows == 0
    cos, sin_signed = _rope_tables(t)
    cbias = _na_col_bias(rpb)
    full = lambda col: pl.BlockSpec((t, HEAD_DIM), lambda h, r: (0, col + h))
    ctx = lambda col: pl.BlockSpec((t_ctx, HEAD_DIM), lambda h, r: (0, col + h))
    tab_q = pl.BlockSpec((nq, HEAD_DIM), lambda h, r: (r, 0))
    tab_k = pl.BlockSpec((t, HEAD_DIM), lambda h, r: (0, 0))
    return pl.pallas_call(
        functools.partial(_na_kernel, rows=rows, prep_rows=prep_rows),
        grid=(n_heads, rows // NA_QROWS),
        in_specs=[pl.BlockSpec((nq, HEAD_DIM), lambda h, r: (r, col0 + h)),
                  full(col0 + n_heads), full(col0 + 2 * n_heads),
                  ctx(col0 + n_heads), ctx(col0 + 2 * n_heads),
                  tab_q, tab_q, tab_k, tab_k,
                  pl.BlockSpec((None, 2 * NA_KH - 1, GRID_W, GRID_W), lambda h, r: (h, 0, 0, 0))],
        out_specs=pl.BlockSpec((nq, HEAD_DIM), lambda h, r: (r, h)),
        out_shape=jax.ShapeDtypeStruct((t, n_heads * HEAD_DIM), BF16),
        scratch_shapes=[pltpu.VMEM((t, HEAD_DIM), BF16), pltpu.VMEM((t, HEAD_DIM), BF16),
                        pltpu.VMEM((3, nq, NA_KROWS * GRID_W), F32)],
        compiler_params=_params("parallel", "arbitrary"),
        name="natten",
    )(p_lat, p_lat, p_lat, p_ctx, p_ctx, cos, sin_signed, cos, sin_signed, cbias)


def _outproj_kernel(hg_ref, na_ref, h_ref, gate_ref, w_ref, o_ref):
    hw = hg_ref.shape[1]
    mix = _dot(hg_ref[...], w_ref[:hw, :]) + _dot(na_ref[...], w_ref[hw:, :])
    o_ref[...] = h_ref[...] + gate_ref[...] * mix


def _outproj_call(hg, na, h, gate, w, tm=512):
    t, d = h.shape
    hw, nw = hg.shape[1], na.shape[1]
    assert t % tm == 0
    return pl.pallas_call(
        _outproj_kernel,
        grid=(t // tm,),
        in_specs=[pl.BlockSpec((tm, hw), lambda i: (i, 0)),
                  pl.BlockSpec((tm, nw), lambda i: (i, 0)),
                  pl.BlockSpec((tm, d), lambda i: (i, 0)),
                  pl.BlockSpec((1, d), lambda i: (0, 0)),
                  pl.BlockSpec((hw + nw, d), lambda i: (0, 0))],
        out_specs=pl.BlockSpec((tm, d), lambda i: (i, 0)),
        out_shape=jax.ShapeDtypeStruct((t, d), F32),
        compiler_params=_params("parallel"),
        name="outproj",
    )(hg, na, h, gate, w)


def kernel(x, c, ctx, c_ctx, w_mod, b_mod, norm_w, ffn1_w_gate, ffn1_w_up, ffn1_w_down, w_in, hg_lb_logits,
           hg_norm_w, na_rpb, w_out, ffn2_w_gate, ffn2_w_up, ffn2_w_down, final_norm_w):
    batch, _, d = x.shape
    depth = w_mod.shape[0]
    assert batch == 1 and depth == 1
    hg_heads = hg_lb_logits.shape[-1] // HEAD_DIM
    na_heads = na_rpb.shape[1]
    bf = lambda w: w.astype(BF16)
    row = lambda v: v.reshape(1, -1)

    cc = jnp.stack([c[0], c_ctx], axis=1)
    mods = _mod_call(cc, w_mod[0], row(b_mod[0]))
    m_l = mods[0].reshape(N_MOD, d)
    m_c = mods[1].reshape(N_MOD, d)
    lb = jnp.cumsum(jax.nn.softmax(hg_lb_logits.astype(F32), axis=1), axis=1)[:, 0]

    ffn1 = (bf(ffn1_w_gate[0]), bf(ffn1_w_up[0]), bf(ffn1_w_down[0]))
    ffn2 = (bf(ffn2_w_gate[0]), bf(ffn2_w_up[0]), bf(ffn2_w_down[0]))
    w_in_b, w_out_b = bf(w_in[0]), bf(w_out[0])

    h_l = _ffn_call(x[0], m_l[0:3], row(norm_w[0, 0]), *ffn1)
    h_c = _ffn_call(ctx[0], m_c[0:3], row(norm_w[0, 0]), *ffn1)
    p_l = _inproj_call(h_l, m_l[3:5], row(norm_w[0, 1]), w_in_b)
    p_c = _inproj_call(h_c, m_c[3:5], row(norm_w[0, 1]), w_in_b)

    o_fw = _hg_call(p_c, p_l, lb[0:1], None, None, rev=False, n_heads=hg_heads)
    hg = _hg_call(p_c, p_l, lb[1:2], o_fw, row(hg_norm_w[0]), rev=True, n_heads=hg_heads)
    na = _na_call(p_c, p_l, na_rpb[0], col0=5 * hg_heads, n_heads=na_heads)

    h_l = _outproj_call(hg, na, h_l, m_l[5:6], w_out_b)
    out = _ffn_call(h_l, m_l[6:9], row(norm_w[0, 2]), *ffn2, final_nw=row(final_norm_w))
    return out[None]
```

```python
import functools

import numpy as np
import jax
import jax.numpy as jnp
from jax import lax
from jax.experimental import pallas as pl
from jax.experimental.pallas import tpu as pltpu

F32 = jnp.float32
BF16 = jnp.bfloat16

EPS = 1e-6
GRID_W = 64
HEAD_DIM = 128
NA_KH = 8
NA_KW = 16
ROPE_BASE = 10000.0
N_MOD = 9
NEG_BIG = -1e30

V7X_VMEM_BYTES = 64 * 1024 * 1024
VMEM_LIMIT = V7X_VMEM_BYTES - 8 * 1024 * 1024

HG_CHUNK = 128
HG_LEVELS = (64, 32, 16, 8, 4, 2, 1)
NA_QROWS = 4
NA_KROWS = NA_QROWS + NA_KH - 1


def _dot(a, b):
    return jnp.dot(a, b, preferred_element_type=F32)


def _dot_nt(a, b):
    return lax.dot_general(a, b, (((1,), (1,)), ((), ())), preferred_element_type=F32)


def _dot_tn(a, b):
    return lax.dot_general(a, b, (((0,), (0,)), ((), ())), preferred_element_type=F32)


def _silu(x):
    return x * jax.nn.sigmoid(x)


def _rmsnorm(x, w):
    return x * lax.rsqrt(jnp.mean(x * x, axis=-1, keepdims=True) + EPS) * w


def _params(*semantics):
    return pltpu.CompilerParams(dimension_semantics=semantics, vmem_limit_bytes=VMEM_LIMIT)


def _mod_kernel(c_ref, w_ref, b_ref, o_ref):
    d, tn = w_ref.shape
    sub = 8

    def body(i, accs):
        rows = pl.ds(pl.multiple_of(i * sub, sub), sub)
        w = w_ref[rows, :]
        sc = _silu(c_ref[rows, :])
        return tuple(acc + w * sc[:, r:r + 1] for r, acc in enumerate(accs))

    zero = jnp.zeros((sub, tn), F32)
    accs = lax.fori_loop(0, d // sub, body, (zero,) * c_ref.shape[1], unroll=8)
    for r, acc in enumerate(accs):
        o_ref[r:r + 1, :] = jnp.sum(acc, axis=0, keepdims=True) + b_ref[...]


def _mod_call(cc, w_mod, b_mod):
    d, n_vec = cc.shape
    n = w_mod.shape[1]
    tn = next(t for t in (1024, 512, 256, 128) if n % t == 0)
    return pl.pallas_call(
        _mod_kernel,
        grid=(n // tn,),
        in_specs=[pl.BlockSpec((d, n_vec), lambda j: (0, 0)),
                  pl.BlockSpec((d, tn), lambda j: (0, j)),
                  pl.BlockSpec((1, tn), lambda j: (0, j))],
        out_specs=pl.BlockSpec((n_vec, tn), lambda j: (0, j)),
        out_shape=jax.ShapeDtypeStruct((n_vec, n), F32),
        compiler_params=_params("parallel"),
        name="mod",
    )(cc, w_mod, b_mod)


def _ffn_kernel(x_ref, mod_ref, nw_ref, wg_ref, wu_ref, wd_ref, *rest, nf, final_norm):
    if final_norm:
        fnw_ref, o_ref, u_scr, acc_scr = rest
    else:
        o_ref, u_scr, acc_scr = rest
    f = pl.program_id(1)

    @pl.when(f == 0)
    def _():
        u = _rmsnorm(x_ref[...], nw_ref[...]) * (1.0 + mod_ref[1:2, :]) + mod_ref[0:1, :]
        u_scr[...] = u.astype(BF16)
        acc_scr[...] = jnp.zeros_like(acc_scr)

    u = u_scr[...]
    a = (_silu(_dot(u, wg_ref[...])) * _dot(u, wu_ref[...])).astype(BF16)
    acc_scr[...] += _dot(a, wd_ref[...])

    @pl.when(f == nf - 1)
    def _():
        h = x_ref[...] + 0.5 * mod_ref[2:3, :] * acc_scr[...]
        if final_norm:
            h = _rmsnorm(h, fnw_ref[...])
        o_ref[...] = h


def _ffn_call(x, mod3, nw, wg, wu, wd, final_nw=None, tm=512, tf=512):
    t, d = x.shape
    dff = wg.shape[1]
    tm = min(tm, t)
    nf = dff // tf
    assert t % tm == 0 and dff % tf == 0
    final_norm = final_nw is not None
    in_specs = [pl.BlockSpec((tm, d), lambda i, f: (i, 0)),
                pl.BlockSpec((3, d), lambda i, f: (0, 0)),
                pl.BlockSpec((1, d), lambda i, f: (0, 0)),
                pl.BlockSpec((d, tf), lambda i, f: (0, f)),
                pl.BlockSpec((d, tf), lambda i, f: (0, f)),
                pl.BlockSpec((tf, d), lambda i, f: (f, 0))]
    args = [x, mod3, nw, wg, wu, wd]
    if final_norm:
        in_specs.append(pl.BlockSpec((1, d), lambda i, f: (0, 0)))
        args.append(final_nw)
    return pl.pallas_call(
        functools.partial(_ffn_kernel, nf=nf, final_norm=final_norm),
        grid=(t // tm, nf),
        in_specs=in_specs,
        out_specs=pl.BlockSpec((tm, d), lambda i, f: (i, 0)),
        out_shape=jax.ShapeDtypeStruct((t, d), F32),
        scratch_shapes=[pltpu.VMEM((tm, d), BF16), pltpu.VMEM((tm, d), F32)],
        compiler_params=_params("parallel", "arbitrary"),
        name="ffn",
    )(*args)


def _inproj_kernel(x_ref, mod_ref, nw_ref, w_ref, o_ref, u_scr):
    @pl.when(pl.program_id(1) == 0)
    def _():
        u = _rmsnorm(x_ref[...], nw_ref[...]) * (1.0 + mod_ref[1:2, :]) + mod_ref[0:1, :]
        u_scr[...] = u.astype(BF16)

    o_ref[...] = _dot(u_scr[...], w_ref[...])


def _inproj_call(x, mod2, nw, w, tm=1024, tn=1024):
    t, d = x.shape
    n = w.shape[1]
    tm = min(tm, t)
    assert t % tm == 0 and n % tn == 0
    return pl.pallas_call(
        _inproj_kernel,
        grid=(t // tm, n // tn),
        in_specs=[pl.BlockSpec((tm, d), lambda i, j: (i, 0)),
                  pl.BlockSpec((2, d), lambda i, j: (0, 0)),
                  pl.BlockSpec((1, d), lambda i, j: (0, 0)),
                  pl.BlockSpec((d, tn), lambda i, j: (0, j))],
        out_specs=pl.BlockSpec((tm, tn), lambda i, j: (i, j)),
        out_shape=jax.ShapeDtypeStruct((t, n), F32),
        scratch_shapes=[pltpu.VMEM((tm, d), BF16)],
        compiler_params=_params("parallel", "arbitrary"),
        name="inproj",
    )(x, mod2, nw, w)


def _split3(x):
    hi = x.astype(BF16)
    r = x - hi.astype(F32)
    mid = r.astype(BF16)
    lo = (r - mid.astype(F32)).astype(BF16)
    return hi, mid, lo


def _hg_gates(q_raw, f_raw, lb, tri):
    q = _silu(q_raw)
    f = lb + (1.0 - lb) * jax.nn.sigmoid(f_raw)
    hi, mid, lo = _split3(jnp.log(f))
    return q, 1.0 - f, f, _dot(tri, hi) + _dot(tri, mid) + _dot(tri, lo)


def _hg_level_operand(m, q, k, f, b, b_row, pos, rev):
    c = HG_CHUNK
    if m >= 8:
        qk, bmid = [], []
        for a in range(c // (2 * m)):
            lo_r, mid_r, hi_r = a * 2 * m, a * 2 * m + m, (a + 1) * 2 * m
            qk += [q[lo_r:mid_r], k[mid_r:hi_r]] if rev else [k[lo_r:mid_r], q[mid_r:hi_r]]
            bmid.append(jnp.broadcast_to(b_row(mid_r if rev else mid_r - 1), (2 * m, HEAD_DIM)))
        qk = jnp.concatenate(qk, axis=0)
        bmid = bmid[0] if len(bmid) == 1 else jnp.concatenate(bmid, axis=0)
        return (qk * jnp.exp(-jnp.abs(b - bmid))).astype(BF16)
    half = pos & (2 * m - 1)
    q_side = (half < m) if rev else (half >= m)
    if m == 4:
        rows = [jnp.broadcast_to(b_row(a * 8 + (4 if rev else 3)), (8, HEAD_DIM)) for a in range(c // 8)]
        w = jnp.exp(-jnp.abs(b - jnp.concatenate(rows, axis=0)))
    elif m == 2:
        f_next = pltpu.roll(f, c - 1, 0)
        f_prev = pltpu.roll(f, 1, 0)
        if rev:
            w = jnp.where(half == 0, f * f_next, jnp.where(half == 1, f, jnp.where(half == 2, 1.0, f_prev)))
        else:
            w = jnp.where(half == 0, f_next, jnp.where(half == 1, 1.0, jnp.where(half == 2, f, f * f_prev)))
    else:
        w = jnp.where(q_side, f, 1.0)
    return (jnp.where(q_side, q, k) * w).astype(BF16)


def _hg_state_step(st, k, vb, b, b_end):
    return st * jnp.exp(b_end) + _dot_tn(vb, (k * jnp.exp(b_end - b)).astype(BF16))


def _hg_ctx_kernel(q_ref, ff_ref, fb_ref, v_ref, lb_ref, tri_ref, o_ref, b_scr):
    c = HG_CHUNK
    n_chunks = q_ref.shape[0] // c
    for d, (f_ref, rev) in enumerate(((ff_ref, False), (fb_ref, True))):
        st = jnp.zeros((HEAD_DIM, HEAD_DIM), F32)
        for ci in (range(n_chunks - 1, -1, -1) if rev else range(n_chunks)):
            sl = pl.ds(ci * c, c)
            _, k, _, b = _hg_gates(q_ref[sl, :], f_ref[sl, :], lb_ref[d:d + 1, :], tri_ref[d])
            b_scr[...] = b
            st = _hg_state_step(st, k, v_ref[sl, :].astype(BF16), b, b_scr[pl.ds(0 if rev else c - 1, 1), :])
        o_ref[d] = st


def _hg_kernel(q_ref, f_ref, v_ref, st0_ref, lb_ref, tri_ref, lvl_ref, *rest, rev, heads_per_step, unroll,
               readout):
    if readout:
        of_ref, g_ref, nw_ref, o_ref, b_scr, st_scr = rest
    else:
        o_ref, b_scr, st_scr = rest
    c = HG_CHUNK

    @pl.when(pl.program_id(1) == 0)
    def _():
        st_scr[...] = st0_ref[...]

    n_chunks = q_ref.shape[0] // c
    tri = tri_ref[...]

    def body(i, carry):
        where = []
        for u in range(unroll):
            ci = i * unroll + u
            ci = (n_chunks - 1 - ci) if rev else ci
            rows = pl.ds(pl.multiple_of(ci * c, c), c)
            where += [(rows, slice(h * HEAD_DIM, (h + 1) * HEAD_DIM)) for h in range(heads_per_step)]
        n = len(where)
        gates = [_hg_gates(q_ref[r, cs], f_ref[r, cs], lb_ref[:, cs], tri) for r, cs in where]
        for p in range(n):
            b_scr[p] = gates[p][3]
        vbs = [v_ref[r, cs].astype(BF16) for r, cs in where]

        pos = lax.broadcasted_iota(jnp.int32, (c, HEAD_DIM), 0)
        lvl = lvl_ref[...]
        scores = [0.0] * n
        for m in HG_LEVELS:
            for p in range(n):
                q, k, f, b = gates[p]
                x = _hg_level_operand(m, q, k, f, b, lambda r, p=p: b_scr[p, pl.ds(r, 1), :], pos, rev)
                scores[p] = jnp.where(lvl == m, _dot_nt(x, x), scores[p])

        outs = [None] * n
        for h in range(heads_per_step):
            st = st_scr[h]
            for u in range(unroll):
                p = u * heads_per_step + h
                q, k, f, b = gates[p]
                outs[p] = _dot_nt((q * jnp.exp(b)).astype(BF16), st.astype(BF16))
                st = _hg_state_step(st, k, vbs[p], b, b_scr[p, pl.ds(0 if rev else c - 1, 1), :])
            st_scr[h] = st

        for p, (r, cs) in enumerate(where):
            q, k, f, b = gates[p]
            o = outs[p] + _dot(scores[p].astype(BF16), vbs[p])
            o = o + jnp.sum(q * k, axis=-1, keepdims=True) * v_ref[r, cs]
            if readout:
                o = _rmsnorm(of_ref[r, cs] + o, nw_ref[...]) * _silu(g_ref[r, cs])
            o_ref[r, cs] = o.astype(o_ref.dtype)
        return carry

    lax.fori_loop(0, n_chunks // unroll, body, 0)


def _hg_tables(rev):
    t = np.arange(HG_CHUNK)[:, None]
    s = np.arange(HG_CHUNK)[None, :]
    x = t ^ s
    msb = np.where(x > 0, 2 ** np.floor(np.log2(np.maximum(x, 1))).astype(np.int64), 0)
    lvl = np.where((s > t) if rev else (s < t), msb, 0).astype(np.int32)
    tri = np.tril(np.ones((HG_CHUNK, HG_CHUNK), np.float32))
    return (tri.T if rev else tri), lvl


def _hg_ctx_call(p_ctx, lb, *, n_heads):
    t_ctx = p_ctx.shape[0]
    assert t_ctx % HG_CHUNK == 0
    tri = jnp.asarray(np.stack([_hg_tables(False)[0], _hg_tables(True)[0]]), BF16)
    col = lambda c0: pl.BlockSpec((t_ctx, HEAD_DIM), lambda h: (0, c0 + h))
    const = pl.BlockSpec((2, HG_CHUNK, HG_CHUNK), lambda h: (0, 0, 0))
    return pl.pallas_call(
        _hg_ctx_kernel,
        grid=(n_heads,),
        in_specs=[col(0), col(n_heads), col(2 * n_heads), col(3 * n_heads),
                  pl.BlockSpec((2, HEAD_DIM), lambda h: (0, h)), const],
        out_specs=pl.BlockSpec((2, None, HEAD_DIM, HEAD_DIM), lambda h: (0, h, 0, 0)),
        out_shape=jax.ShapeDtypeStruct((2, n_heads, HEAD_DIM, HEAD_DIM), F32),
        scratch_shapes=[pltpu.VMEM((HG_CHUNK, HEAD_DIM), F32)],
        compiler_params=_params("parallel"),
        name="hgrn_ctx",
    )(p_ctx, p_ctx, p_ctx, p_ctx, lb, tri)


def _hg_call(p_lat, st0, lb_dir, o_fw, norm_w, *, rev, n_heads, tb=1024, heads_per_step=2, unroll=2):
    t_lat = p_lat.shape[0]
    tb = min(tb, t_lat)
    hb = heads_per_step
    assert t_lat % tb == 0 and tb % (HG_CHUNK * unroll) == 0 and n_heads % hb == 0
    n_blk, n_grp = t_lat // tb, n_heads // hb
    readout = o_fw is not None
    blk_row = (lambda j: n_blk - 1 - j) if rev else (lambda j: j)
    lat = lambda c0: pl.BlockSpec((tb, hb * HEAD_DIM), lambda g, j: (blk_row(j), c0 * n_grp + g))
    tri, lvl = _hg_tables(rev)
    const = pl.BlockSpec((HG_CHUNK, HG_CHUNK), lambda g, j: (0, 0))
    in_specs = [lat(0), lat(2 if rev else 1), lat(3),
                pl.BlockSpec((hb, HEAD_DIM, HEAD_DIM), lambda g, j: (g, 0, 0)),
                pl.BlockSpec((1, hb * HEAD_DIM), lambda g, j: (0, g)), const, const]
    args = [p_lat, p_lat, p_lat, st0, lb_dir, jnp.asarray(tri, BF16), jnp.asarray(lvl)]
    out_spec = pl.BlockSpec((tb, hb * HEAD_DIM), lambda g, j: (blk_row(j), g))
    if readout:
        in_specs += [out_spec, lat(4), pl.BlockSpec((1, HEAD_DIM), lambda g, j: (0, 0))]
        args += [o_fw, p_lat, norm_w]
    return pl.pallas_call(
        functools.partial(_hg_kernel, rev=rev, heads_per_step=hb, unroll=unroll, readout=readout),
        grid=(n_grp, n_blk),
        in_specs=in_specs,
        out_specs=out_spec,
        out_shape=jax.ShapeDtypeStruct((t_lat, n_heads * HEAD_DIM), BF16 if readout else F32),
        scratch_shapes=[pltpu.VMEM((unroll * hb, HG_CHUNK, HEAD_DIM), F32),
                        pltpu.VMEM((hb, HEAD_DIM, HEAD_DIM), F32)],
        compiler_params=_params("parallel", "arbitrary"),
        name="hgrn_bwd" if rev else "hgrn_fwd",
    )(*args)


def _rope(x, cos, sin_signed, low):
    return x * cos + jnp.where(low, pltpu.roll(x, 96, 1), pltpu.roll(x, 32, 1)) * sin_signed


def _na_kernel(q_ref, k_ref, v_ref, kc_ref, vc_ref, cosq_ref, sinq_ref, cosk_ref, sink_ref, cbias_ref,
               o_ref, k_scr, v_scr, bias_scr, *, rows, prep_rows):
    rb = pl.program_id(1)
    n_rb = pl.num_programs(1)
    t = k_ref.shape[0]

    def low_mask(n):
        lane = lax.broadcasted_iota(jnp.int32, (n, HEAD_DIM), 1)
        return (lane & (HEAD_DIM // 2 - 1)) < HEAD_DIM // 4

    @pl.when(rb == 0)
    def _():
        low = low_mask(prep_rows)

        def body(i, carry):
            sl = pl.ds(pl.multiple_of(i * prep_rows, prep_rows), prep_rows)
            k_scr[sl, :] = _rope(k_ref[sl, :], cosk_ref[sl, :], sink_ref[sl, :], low).astype(BF16)
            v_scr[sl, :] = v_ref[sl, :].astype(BF16)
            return carry

        lax.fori_loop(0, t // prep_rows, body, 0)

        bias_scr[...] = jnp.full(bias_scr.shape, NEG_BIG, F32)
        for e in range(3):
            for a in range(NA_QROWS):
                b0 = (0, a, NA_KROWS - NA_KH)[e]
                d0 = (NA_KH - 1 - a, NA_KH // 2 - 1, NA_QROWS - 1 - a)[e]
                for i in range(NA_KH):
                    bias_scr[e, a * GRID_W:(a + 1) * GRID_W, (b0 + i) * GRID_W:(b0 + i + 1) * GRID_W] = (
                        cbias_ref[d0 + i])

    nq = q_ref.shape[0]
    q = _rope(q_ref[...], cosq_ref[...], sinq_ref[...], low_mask(nq)) * (HEAD_DIM ** -0.5)
    q = q.astype(BF16)
    key_row0 = jnp.clip(rb * NA_QROWS - NA_KH // 2, 0, rows - NA_KROWS)
    win = pl.ds(pl.multiple_of(key_row0 * GRID_W, GRID_W), NA_KROWS * GRID_W)
    kw = k_scr[win, :]
    vw = v_scr[win, :]
    edge = jnp.where(rb == 0, 0, jnp.where(rb == n_rb - 1, 2, 1))
    s_w = _dot_nt(q, kw) + bias_scr[edge]
    s_c = _dot_nt(q, kc_ref[...].astype(BF16))
    m = jnp.maximum(jnp.max(s_w, axis=-1, keepdims=True), jnp.max(s_c, axis=-1, keepdims=True))
    p_w = jnp.exp(s_w - m)
    p_c = jnp.exp(s_c - m)
    denom = jnp.sum(p_w, axis=-1, keepdims=True) + jnp.sum(p_c, axis=-1, keepdims=True)
    o = _dot(p_w.astype(BF16), vw) + _dot(p_c.astype(BF16), vc_ref[...].astype(BF16))
    o_ref[...] = (o / denom).astype(o_ref.dtype)


def _na_col_bias(rpb):
    col = np.arange(GRID_W)
    col_start = np.clip(col - NA_KW // 2, 0, GRID_W - NA_KW)
    col_in = (col[None, :] >= col_start[:, None]) & (col[None, :] < col_start[:, None] + NA_KW)
    dc = np.clip(col[None, :] - col[:, None], 1 - NA_KW, NA_KW - 1) + (NA_KW - 1)
    onehot = (dc[None] == np.arange(2 * NA_KW - 1)[:, None, None]).astype(np.float32)
    vals = jnp.einsum("hrk,kqc->hrqc", rpb.astype(F32), onehot, precision=lax.Precision.HIGHEST)
    return jnp.where(col_in[None, None], vals, NEG_BIG)


def _rope_tables(n_tokens):
    t = jnp.arange(n_tokens)
    row = (t // GRID_W).astype(F32)
    col = (t % GRID_W).astype(F32)
    quarter = HEAD_DIM // 4
    inv = ROPE_BASE ** (-jnp.arange(quarter, dtype=F32) / quarter)
    ang_r = row[:, None] * inv[None, :]
    ang_c = col[:, None] * inv[None, :]
    ang = jnp.concatenate([ang_r, ang_r, ang_c, ang_c], axis=-1)
    sign = np.where((np.arange(HEAD_DIM) % (HEAD_DIM // 2)) < quarter, -1.0, 1.0).astype(np.float32)
    return jnp.cos(ang), jnp.sin(ang) * sign


def _na_call(p_ctx, p_lat, rpb, *, col0, n_heads):
    t_ctx, t = p_ctx.shape[0], p_lat.shape[0]
    rows = t // GRID_W
    assert t % GRID_W == 0 and rows % NA_QROWS == 0 and rows >= NA_KROWS + NA_QROWS
    nq = NA_QROWS * GRID_W
    prep_rows = 512
    assert t % prep_rows == 0
    cos, sin_signed = _rope_tables(t)
    cbias = _na_col_bias(rpb)
    full = lambda col: pl.BlockSpec((t, HEAD_DIM), lambda h, r: (0, col + h))
    ctx = lambda col: pl.BlockSpec((t_ctx, HEAD_DIM), lambda h, r: (0, col + h))
    tab_q = pl.BlockSpec((nq, HEAD_DIM), lambda h, r: (r, 0))
    tab_k = pl.BlockSpec((t, HEAD_DIM), lambda h, r: (0, 0))
    return pl.pallas_call(
        functools.partial(_na_kernel, rows=rows, prep_rows=prep_rows),
        grid=(n_heads, rows // NA_QROWS),
        in_specs=[pl.BlockSpec((nq, HEAD_DIM), lambda h, r: (r, col0 + h)),
                  full(col0 + n_heads), full(col0 + 2 * n_heads),
                  ctx(col0 + n_heads), ctx(col0 + 2 * n_heads),
                  tab_q, tab_q, tab_k, tab_k,
                  pl.BlockSpec((None, 2 * NA_KH - 1, GRID_W, GRID_W), lambda h, r: (h, 0, 0, 0))],
        out_specs=pl.BlockSpec((nq, HEAD_DIM), lambda h, r: (r, h)),
        out_shape=jax.ShapeDtypeStruct((t, n_heads * HEAD_DIM), BF16),
        scratch_shapes=[pltpu.VMEM((t, HEAD_DIM), BF16), pltpu.VMEM((t, HEAD_DIM), BF16),
                        pltpu.VMEM((3, nq, NA_KROWS * GRID_W), F32)],
        compiler_params=_params("parallel", "arbitrary"),
        name="natten",
    )(p_lat, p_lat, p_lat, p_ctx, p_ctx, cos, sin_signed, cos, sin_signed, cbias)


def _outproj_kernel(hg_ref, na_ref, h_ref, gate_ref, w_ref, o_ref):
    hw = hg_ref.shape[1]
    mix = _dot(hg_ref[...], w_ref[:hw, :]) + _dot(na_ref[...], w_ref[hw:, :])
    o_ref[...] = h_ref[...] + gate_ref[...] * mix


def _outproj_call(hg, na, h, gate, w, tm=512):
    t, d = h.shape
    hw, nw = hg.shape[1], na.shape[1]
    assert t % tm == 0
    return pl.pallas_call(
        _outproj_kernel,
        grid=(t // tm,),
        in_specs=[pl.BlockSpec((tm, hw), lambda i: (i, 0)),
                  pl.BlockSpec((tm, nw), lambda i: (i, 0)),
                  pl.BlockSpec((tm, d), lambda i: (i, 0)),
                  pl.BlockSpec((1, d), lambda i: (0, 0)),
                  pl.BlockSpec((hw + nw, d), lambda i: (0, 0))],
        out_specs=pl.BlockSpec((tm, d), lambda i: (i, 0)),
        out_shape=jax.ShapeDtypeStruct((t, d), F32),
        compiler_params=_params("parallel"),
        name="outproj",
    )(hg, na, h, gate, w)


def kernel(x, c, ctx, c_ctx, w_mod, b_mod, norm_w, ffn1_w_gate, ffn1_w_up, ffn1_w_down, w_in, hg_lb_logits,
           hg_norm_w, na_rpb, w_out, ffn2_w_gate, ffn2_w_up, ffn2_w_down, final_norm_w):
    batch, _, d = x.shape
    depth = w_mod.shape[0]
    assert batch == 1 and depth == 1
    hg_heads = hg_lb_logits.shape[-1] // HEAD_DIM
    na_heads = na_rpb.shape[1]
    bf = lambda w: w.astype(BF16)
    row = lambda v: v.reshape(1, -1)

    cc = jnp.stack([c[0], c_ctx], axis=1)
    mods = _mod_call(cc, w_mod[0], row(b_mod[0]))
    m_l = mods[0].reshape(N_MOD, d)
    m_c = mods[1].reshape(N_MOD, d)
    lb = jnp.cumsum(jax.nn.softmax(hg_lb_logits.astype(F32), axis=1), axis=1)[:, 0]

    ffn1 = (bf(ffn1_w_gate[0]), bf(ffn1_w_up[0]), bf(ffn1_w_down[0]))
    ffn2 = (bf(ffn2_w_gate[0]), bf(ffn2_w_up[0]), bf(ffn2_w_down[0]))
    w_in_b, w_out_b = bf(w_in[0]), bf(w_out[0])

    h_l = _ffn_call(x[0], m_l[0:3], row(norm_w[0, 0]), *ffn1)
    h_c = _ffn_call(ctx[0], m_c[0:3], row(norm_w[0, 0]), *ffn1)
    p_l = _inproj_call(h_l, m_l[3:5], row(norm_w[0, 1]), w_in_b)
    p_c = _inproj_call(h_c, m_c[3:5], row(norm_w[0, 1]), w_in_b)

    st0 = _hg_ctx_call(p_c, lb, n_heads=hg_heads)
    o_fw = _hg_call(p_l, st0[0], lb[0:1], None, None, rev=False, n_heads=hg_heads)
    hg = _hg_call(p_l, st0[1], lb[1:2], o_fw, row(hg_norm_w[0]), rev=True, n_heads=hg_heads)
    na = _na_call(p_c, p_l, na_rpb[0], col0=5 * hg_heads, n_heads=na_heads)

    h_l = _outproj_call(hg, na, h_l, m_l[5:6], w_out_b)
    out = _ffn_call(h_l, m_l[6:9], row(norm_w[0, 2]), *ffn2, final_nw=row(final_norm_w))
    return out[None]
```

```python
import functools

import numpy as np
import jax
import jax.numpy as jnp
from jax import lax
from jax.experimental import pallas as pl
from jax.experimental.pallas import tpu as pltpu

F32 = jnp.float32
BF16 = jnp.bfloat16

EPS = 1e-6
GRID_W = 64
HEAD_DIM = 128
NA_KH = 8
NA_KW = 16
ROPE_BASE = 10000.0
N_MOD = 9
NEG_BIG = -1e30

V7X_VMEM_BYTES = 64 * 1024 * 1024
VMEM_LIMIT = V7X_VMEM_BYTES - 8 * 1024 * 1024

HG_CHUNK = 128
HG_LEVELS = (64, 32, 16, 8, 4, 2, 1)
NA_QROWS = 4
NA_KROWS = NA_QROWS + NA_KH
NA_STEP_BLOCKS = 4
NA_PREP_TILES = 4


def _dot(a, b):
    return jnp.dot(a, b, preferred_element_type=F32)


def _dot_nt(a, b):
    return lax.dot_general(a, b, (((1,), (1,)), ((), ())), preferred_element_type=F32)


def _dot_tn(a, b):
    return lax.dot_general(a, b, (((0,), (0,)), ((), ())), preferred_element_type=F32)


def _silu(x):
    return x * jax.nn.sigmoid(x)


def _rmsnorm(x, w):
    return x * lax.rsqrt(jnp.mean(x * x, axis=-1, keepdims=True) + EPS) * w


def _params(*semantics):
    return pltpu.CompilerParams(dimension_semantics=semantics, vmem_limit_bytes=VMEM_LIMIT)


def _mod_kernel(c_ref, w_ref, b_ref, o_ref, sc_scr):
    d, tn = w_ref.shape
    n_vec = c_ref.shape[1]
    sub, lanes = 8, sc_scr.shape[-1]

    @pl.when(pl.program_id(0) == 0)
    def _():
        sc = _silu(c_ref[...])
        for r in range(n_vec):
            sc_scr[r] = jnp.broadcast_to(sc[:, r:r + 1], (d, lanes))

    def body(i, accs):
        rows = pl.ds(pl.multiple_of(i * sub, sub), sub)
        w = w_ref[rows, :]
        return tuple(acc + w * jnp.concatenate([sc_scr[r, rows, :]] * (tn // lanes), axis=1)
                     for r, acc in enumerate(accs))

    zero = jnp.zeros((sub, tn), F32)
    accs = lax.fori_loop(0, d // sub, body, (zero,) * n_vec, unroll=8)
    for r, acc in enumerate(accs):
        o_ref[r:r + 1, :] = jnp.sum(acc, axis=0, keepdims=True) + b_ref[...]


def _mod_call(cc, w_mod, b_mod):
    d, n_vec = cc.shape
    n = w_mod.shape[1]
    tn = next(t for t in (1024, 512, 256, 128) if n % t == 0)
    return pl.pallas_call(
        _mod_kernel,
        grid=(n // tn,),
        in_specs=[pl.BlockSpec((d, n_vec), lambda j: (0, 0)),
                  pl.BlockSpec((d, tn), lambda j: (0, j)),
                  pl.BlockSpec((1, tn), lambda j: (0, j))],
        out_specs=pl.BlockSpec((n_vec, tn), lambda j: (0, j)),
        out_shape=jax.ShapeDtypeStruct((n_vec, n), F32),
        scratch_shapes=[pltpu.VMEM((n_vec, d, HEAD_DIM), F32)],
        compiler_params=_params("arbitrary"),
        name="mod",
    )(cc, w_mod, b_mod)


def _ffn_kernel(x_ref, mod_ref, nw_ref, wg_ref, wu_ref, wd_ref, *rest, nf, final_norm):
    if final_norm:
        fnw_ref, o_ref, u_scr, acc_scr = rest
    else:
        o_ref, u_scr, acc_scr = rest
    f = pl.program_id(1)

    @pl.when(f == 0)
    def _():
        u = _rmsnorm(x_ref[...], nw_ref[...]) * (1.0 + mod_ref[1:2, :]) + mod_ref[0:1, :]
        u_scr[...] = u.astype(BF16)
        acc_scr[...] = jnp.zeros_like(acc_scr)

    u = u_scr[...]
    a = (_silu(_dot(u, wg_ref[...])) * _dot(u, wu_ref[...])).astype(BF16)
    acc_scr[...] += _dot(a, wd_ref[...])

    @pl.when(f == nf - 1)
    def _():
        h = x_ref[...] + 0.5 * mod_ref[2:3, :] * acc_scr[...]
        if final_norm:
            h = _rmsnorm(h, fnw_ref[...])
        o_ref[...] = h


def _ffn_call(x, mod3, nw, wg, wu, wd, final_nw=None, tm=512, tf=512):
    t, d = x.shape
    dff = wg.shape[1]
    tm = min(tm, t)
    nf = dff // tf
    assert t % tm == 0 and dff % tf == 0
    final_norm = final_nw is not None
    in_specs = [pl.BlockSpec((tm, d), lambda i, f: (i, 0)),
                pl.BlockSpec((3, d), lambda i, f: (0, 0)),
                pl.BlockSpec((1, d), lambda i, f: (0, 0)),
                pl.BlockSpec((d, tf), lambda i, f: (0, f)),
                pl.BlockSpec((d, tf), lambda i, f: (0, f)),
                pl.BlockSpec((tf, d), lambda i, f: (f, 0))]
    args = [x, mod3, nw, wg, wu, wd]
    if final_norm:
        in_specs.append(pl.BlockSpec((1, d), lambda i, f: (0, 0)))
        args.append(final_nw)
    return pl.pallas_call(
        functools.partial(_ffn_kernel, nf=nf, final_norm=final_norm),
        grid=(t // tm, nf),
        in_specs=in_specs,
        out_specs=pl.BlockSpec((tm, d), lambda i, f: (i, 0)),
        out_shape=jax.ShapeDtypeStruct((t, d), F32),
        scratch_shapes=[pltpu.VMEM((tm, d), BF16), pltpu.VMEM((tm, d), F32)],
        compiler_params=_params("parallel", "arbitrary"),
        name="ffn",
    )(*args)


def _inproj_kernel(x_ref, mod_ref, nw_ref, w_ref, o_ref, u_scr):
    @pl.when(pl.program_id(1) == 0)
    def _():
        u = _rmsnorm(x_ref[...], nw_ref[...]) * (1.0 + mod_ref[1:2, :]) + mod_ref[0:1, :]
        u_scr[...] = u.astype(BF16)

    o_ref[...] = _dot(u_scr[...], w_ref[...])


def _inproj_call(x, mod2, nw, w, tm=1024, tn=1024):
    t, d = x.shape
    n = w.shape[1]
    tm = min(tm, t)
    assert t % tm == 0 and n % tn == 0
    return pl.pallas_call(
        _inproj_kernel,
        grid=(t // tm, n // tn),
        in_specs=[pl.BlockSpec((tm, d), lambda i, j: (i, 0)),
                  pl.BlockSpec((2, d), lambda i, j: (0, 0)),
                  pl.BlockSpec((1, d), lambda i, j: (0, 0)),
                  pl.BlockSpec((d, tn), lambda i, j: (0, j))],
        out_specs=pl.BlockSpec((tm, tn), lambda i, j: (i, j)),
        out_shape=jax.ShapeDtypeStruct((t, n), F32),
        scratch_shapes=[pltpu.VMEM((tm, d), BF16)],
        compiler_params=_params("parallel", "arbitrary"),
        name="inproj",
    )(x, mod2, nw, w)


def _split3(x):
    hi = x.astype(BF16)
    r = x - hi.astype(F32)
    mid = r.astype(BF16)
    lo = (r - mid.astype(F32)).astype(BF16)
    return hi, mid, lo


def _hg_gates(q_raw, f_raw, lb, tri):
    q = _silu(q_raw)
    f = lb + (1.0 - lb) * jax.nn.sigmoid(f_raw)
    hi, mid, lo = _split3(jnp.log(f))
    return q, 1.0 - f, f, _dot(tri, hi) + _dot(tri, mid) + _dot(tri, lo)


def _hg_level_operand(m, q, k, f, b, b_row, pos, rev):
    c = HG_CHUNK
    if m >= 8:
        qk, bmid = [], []
        for a in range(c // (2 * m)):
            lo_r, mid_r, hi_r = a * 2 * m, a * 2 * m + m, (a + 1) * 2 * m
            qk += [q[lo_r:mid_r], k[mid_r:hi_r]] if rev else [k[lo_r:mid_r], q[mid_r:hi_r]]
            bmid.append(jnp.broadcast_to(b_row(mid_r if rev else mid_r - 1), (2 * m, HEAD_DIM)))
        qk = jnp.concatenate(qk, axis=0)
        bmid = bmid[0] if len(bmid) == 1 else jnp.concatenate(bmid, axis=0)
        return (qk * jnp.exp(-jnp.abs(b - bmid))).astype(BF16)
    half = pos & (2 * m - 1)
    q_side = (half < m) if rev else (half >= m)
    if m == 4:
        rows = [jnp.broadcast_to(b_row(a * 8 + (4 if rev else 3)), (8, HEAD_DIM)) for a in range(c // 8)]
        w = jnp.exp(-jnp.abs(b - jnp.concatenate(rows, axis=0)))
    elif m == 2:
        f_next = pltpu.roll(f, c - 1, 0)
        f_prev = pltpu.roll(f, 1, 0)
        if rev:
            w = jnp.where(half == 0, f * f_next, jnp.where(half == 1, f, jnp.where(half == 2, 1.0, f_prev)))
        else:
            w = jnp.where(half == 0, f_next, jnp.where(half == 1, 1.0, jnp.where(half == 2, f, f * f_prev)))
    else:
        w = jnp.where(q_side, f, 1.0)
    return (jnp.where(q_side, q, k) * w).astype(BF16)


def _hg_state_step(st, k, vb, b, b_end):
    return st * jnp.exp(b_end) + _dot_tn(vb, (k * jnp.exp(b_end - b)).astype(BF16))


def _hg_ctx_kernel(q_ref, ff_ref, fb_ref, v_ref, lb_ref, tri_ref, o_ref, b_scr):
    c = HG_CHUNK
    n_chunks = q_ref.shape[0] // c
    for d, (f_ref, rev) in enumerate(((ff_ref, False), (fb_ref, True))):
        st = jnp.zeros((HEAD_DIM, HEAD_DIM), F32)
        for ci in (range(n_chunks - 1, -1, -1) if rev else range(n_chunks)):
            sl = pl.ds(ci * c, c)
            _, k, _, b = _hg_gates(q_ref[sl, :], f_ref[sl, :], lb_ref[d:d + 1, :], tri_ref[d])
            b_scr[...] = b
            st = _hg_state_step(st, k, v_ref[sl, :].astype(BF16), b, b_scr[pl.ds(0 if rev else c - 1, 1), :])
        o_ref[d] = st


def _hg_kernel(q_ref, f_ref, v_ref, st0_ref, lb_ref, tri_ref, lvl_ref, *rest, rev, heads_per_step, unroll,
               readout):
    if readout:
        of_ref, g_ref, nw_ref, o_ref, b_scr, st_scr = rest
    else:
        o_ref, b_scr, st_scr = rest
    c = HG_CHUNK

    @pl.when(pl.program_id(1) == 0)
    def _():
        st_scr[...] = st0_ref[...]

    n_chunks = q_ref.shape[0] // c
    tri = tri_ref[...]

    def body(i, carry):
        where = []
        for u in range(unroll):
            ci = i * unroll + u
            ci = (n_chunks - 1 - ci) if rev else ci
            rows = pl.ds(pl.multiple_of(ci * c, c), c)
            where += [(rows, slice(h * HEAD_DIM, (h + 1) * HEAD_DIM)) for h in range(heads_per_step)]
        n = len(where)
        gates = [_hg_gates(q_ref[r, cs], f_ref[r, cs], lb_ref[:, cs], tri) for r, cs in where]
        for p in range(n):
            b_scr[p] = gates[p][3]
        vbs = [v_ref[r, cs].astype(BF16) for r, cs in where]

        pos = lax.broadcasted_iota(jnp.int32, (c, HEAD_DIM), 0)
        lvl = lvl_ref[...]
        scores = [0.0] * n
        for m in HG_LEVELS:
            for p in range(n):
                q, k, f, b = gates[p]
                x = _hg_level_operand(m, q, k, f, b, lambda r, p=p: b_scr[p, pl.ds(r, 1), :], pos, rev)
                scores[p] = jnp.where(lvl == m, _dot_nt(x, x), scores[p])

        outs = [None] * n
        for h in range(heads_per_step):
            st = st_scr[h]
            for u in range(unroll):
                p = u * heads_per_step + h
                q, k, f, b = gates[p]
                outs[p] = _dot_nt((q * jnp.exp(b)).astype(BF16), st.astype(BF16))
                st = _hg_state_step(st, k, vbs[p], b, b_scr[p, pl.ds(0 if rev else c - 1, 1), :])
            st_scr[h] = st

        for p, (r, cs) in enumerate(where):
            q, k, f, b = gates[p]
            o = outs[p] + _dot(scores[p].astype(BF16), vbs[p])
            o = o + jnp.sum(q * k, axis=-1, keepdims=True) * v_ref[r, cs]
            if readout:
                o = _rmsnorm(of_ref[r, cs] + o, nw_ref[...]) * _silu(g_ref[r, cs])
            o_ref[r, cs] = o.astype(o_ref.dtype)
        return carry

    lax.fori_loop(0, n_chunks // unroll, body, 0)


def _hg_tables(rev):
    t = np.arange(HG_CHUNK)[:, None]
    s = np.arange(HG_CHUNK)[None, :]
    x = t ^ s
    msb = np.where(x > 0, 2 ** np.floor(np.log2(np.maximum(x, 1))).astype(np.int64), 0)
    lvl = np.where((s > t) if rev else (s < t), msb, 0).astype(np.int32)
    tri = np.tril(np.ones((HG_CHUNK, HG_CHUNK), np.float32))
    return (tri.T if rev else tri), lvl


def _hg_ctx_call(p_ctx, lb, *, n_heads):
    t_ctx = p_ctx.shape[0]
    assert t_ctx % HG_CHUNK == 0
    tri = jnp.asarray(np.stack([_hg_tables(False)[0], _hg_tables(True)[0]]), BF16)
    col = lambda c0: pl.BlockSpec((t_ctx, HEAD_DIM), lambda h: (0, c0 + h))
    const = pl.BlockSpec((2, HG_CHUNK, HG_CHUNK), lambda h: (0, 0, 0))
    return pl.pallas_call(
        _hg_ctx_kernel,
        grid=(n_heads,),
        in_specs=[col(0), col(n_heads), col(2 * n_heads), col(3 * n_heads),
                  pl.BlockSpec((2, HEAD_DIM), lambda h: (0, h)), const],
        out_specs=pl.BlockSpec((2, None, HEAD_DIM, HEAD_DIM), lambda h: (0, h, 0, 0)),
        out_shape=jax.ShapeDtypeStruct((2, n_heads, HEAD_DIM, HEAD_DIM), F32),
        scratch_shapes=[pltpu.VMEM((HG_CHUNK, HEAD_DIM), F32)],
        compiler_params=_params("parallel"),
        name="hgrn_ctx",
    )(p_ctx, p_ctx, p_ctx, p_ctx, lb, tri)


def _hg_call(p_lat, st0, lb_dir, o_fw, norm_w, *, rev, n_heads, tb=1024, heads_per_step=2, unroll=2):
    t_lat = p_lat.shape[0]
    tb = min(tb, t_lat)
    hb = heads_per_step
    assert t_lat % tb == 0 and tb % (HG_CHUNK * unroll) == 0 and n_heads % hb == 0
    n_blk, n_grp = t_lat // tb, n_heads // hb
    readout = o_fw is not None
    blk_row = (lambda j: n_blk - 1 - j) if rev else (lambda j: j)
    lat = lambda c0: pl.BlockSpec((tb, hb * HEAD_DIM), lambda g, j: (blk_row(j), c0 * n_grp + g))
    tri, lvl = _hg_tables(rev)
    const = pl.BlockSpec((HG_CHUNK, HG_CHUNK), lambda g, j: (0, 0))
    in_specs = [lat(0), lat(2 if rev else 1), lat(3),
                pl.BlockSpec((hb, HEAD_DIM, HEAD_DIM), lambda g, j: (g, 0, 0)),
                pl.BlockSpec((1, hb * HEAD_DIM), lambda g, j: (0, g)), const, const]
    args = [p_lat, p_lat, p_lat, st0, lb_dir, jnp.asarray(tri, BF16), jnp.asarray(lvl)]
    out_spec = pl.BlockSpec((tb, hb * HEAD_DIM), lambda g, j: (blk_row(j), g))
    if readout:
        in_specs += [out_spec, lat(4), pl.BlockSpec((1, HEAD_DIM), lambda g, j: (0, 0))]
        args += [o_fw, p_lat, norm_w]
    return pl.pallas_call(
        functools.partial(_hg_kernel, rev=rev, heads_per_step=hb, unroll=unroll, readout=readout),
        grid=(n_grp, n_blk),
        in_specs=in_specs,
        out_specs=out_spec,
        out_shape=jax.ShapeDtypeStruct((t_lat, n_heads * HEAD_DIM), BF16 if readout else F32),
        scratch_shapes=[pltpu.VMEM((unroll * hb, HG_CHUNK, HEAD_DIM), F32),
                        pltpu.VMEM((hb, HEAD_DIM, HEAD_DIM), F32)],
        compiler_params=_params("parallel", "arbitrary"),
        name="hgrn_bwd" if rev else "hgrn_fwd",
    )(*args)


def _rope(x, cos, sin_signed, low):
    return x * cos + jnp.where(low, pltpu.roll(x, 96, 1), pltpu.roll(x, 32, 1)) * sin_signed


def _na_kernel(q_ref, k_ref, v_ref, kc_ref, vc_ref, rtab_ref, ctab_ref, cbias_ref,
               o_ref, kt_scr, v_scr, kct_scr, bias_scr, *, rows):
    tile = 2 * GRID_W
    n_rb = rows // NA_QROWS
    lane = lax.broadcasted_iota(jnp.int32, (GRID_W, HEAD_DIM), 1)
    low = (lane & (HEAD_DIM // 2 - 1)) < HEAD_DIM // 4

    def rope_rows(x_ref, row0, n):
        out = []
        for i in range(n):
            x = x_ref[pl.ds(pl.multiple_of((row0 + i) * GRID_W, GRID_W), GRID_W), :]
            cos = rtab_ref[0, pl.ds(row0 + i, 1), :] + ctab_ref[0]
            sin = rtab_ref[1, pl.ds(row0 + i, 1), :] + ctab_ref[1]
            out.append(_rope(x, cos, sin, low))
        return jnp.concatenate(out, axis=0)

    def prep(i, carry):
        js = [i * NA_PREP_TILES + u for u in range(NA_PREP_TILES)]
        roped = [rope_rows(k_ref, 2 * j, 2) for j in js]
        for j, kr in zip(js, roped):
            kt_scr[j] = kr.T.astype(BF16)
            v_scr[j] = v_ref[pl.ds(pl.multiple_of(j * tile, tile), tile), :].astype(BF16)
        return carry

    lax.fori_loop(0, rows // (2 * NA_PREP_TILES), prep, 0)
    kct_scr[...] = kc_ref[...].T.astype(BF16)

    bias_scr[...] = jnp.full(bias_scr.shape, NEG_BIG, F32)
    for e in range(3):
        for a in range(NA_QROWS):
            b0 = (0, a, NA_KROWS - NA_KH)[e]
            d0 = (NA_KH - 1 - a, NA_KH // 2 - 1, NA_QROWS - 1 - a)[e]
            for i in range(NA_KH):
                bias_scr[e, a * GRID_W:(a + 1) * GRID_W, (b0 + i) * GRID_W:(b0 + i + 1) * GRID_W] = (
                    cbias_ref[d0 + i])

    vc = vc_ref[...].astype(BF16)

    def row_blocks(i, carry):
        rbs = [i * NA_STEP_BLOCKS + u for u in range(NA_STEP_BLOCKS)]
        qs = [(rope_rows(q_ref, rb * NA_QROWS, NA_QROWS) * (HEAD_DIM ** -0.5)).astype(BF16) for rb in rbs]
        j0s = [jnp.clip(rb * (NA_QROWS // 2) - NA_KH // 4, 0, (rows - NA_KROWS) // 2) for rb in rbs]
        s_w, s_c = [], []
        for rb, q, j0 in zip(rbs, qs, j0s):
            kw = jnp.concatenate([kt_scr[j0 + t] for t in range(NA_KROWS // 2)], axis=1)
            edge = jnp.where(rb == 0, 0, jnp.where(rb == n_rb - 1, 2, 1))
            s_w.append(_dot(q, kw) + bias_scr[edge])
            s_c.append(_dot(q, kct_scr[...]))
        nq = NA_QROWS * GRID_W
        for rb, j0, sw, sc in zip(rbs, j0s, s_w, s_c):
            m = jnp.maximum(jnp.max(sw, axis=-1, keepdims=True), jnp.max(sc, axis=-1, keepdims=True))
            p_w = jnp.exp(sw - m)
            p_c = jnp.exp(sc - m)
            denom = jnp.sum(p_w, axis=-1, keepdims=True) + jnp.sum(p_c, axis=-1, keepdims=True)
            vw = jnp.concatenate([v_scr[j0 + t] for t in range(NA_KROWS // 2)], axis=0)
            o = _dot(p_w.astype(BF16), vw) + _dot(p_c.astype(BF16), vc)
            o_ref[pl.ds(pl.multiple_of(rb * nq, nq), nq), :] = (o / denom).astype(o_ref.dtype)
        return carry

    lax.fori_loop(0, n_rb // NA_STEP_BLOCKS, row_blocks, 0)


def _na_col_bias(rpb):
    col = np.arange(GRID_W)
    col_start = np.clip(col - NA_KW // 2, 0, GRID_W - NA_KW)
    col_in = (col[None, :] >= col_start[:, None]) & (col[None, :] < col_start[:, None] + NA_KW)
    dc = np.clip(col[None, :] - col[:, None], 1 - NA_KW, NA_KW - 1) + (NA_KW - 1)
    onehot = (dc[None] == np.arange(2 * NA_KW - 1)[:, None, None]).astype(np.float32)
    vals = jnp.einsum("hrk,kqc->hrqc", rpb.astype(F32), onehot, precision=lax.Precision.HIGHEST)
    return jnp.where(col_in[None, None], vals, NEG_BIG)


def _rope_tables(rows):
    quarter = HEAD_DIM // 4
    inv = ROPE_BASE ** (-jnp.arange(quarter, dtype=F32) / quarter)
    ang_r = jnp.arange(rows).astype(F32)[:, None] * inv[None, :]
    ang_c = jnp.arange(GRID_W).astype(F32)[:, None] * inv[None, :]
    half = lambda a: jnp.stack([jnp.concatenate([jnp.cos(a), jnp.cos(a)], axis=-1),
                                jnp.concatenate([-jnp.sin(a), jnp.sin(a)], axis=-1)])
    r, c = half(ang_r), half(ang_c)
    return jnp.concatenate([r, jnp.zeros_like(r)], axis=-1), jnp.concatenate([jnp.zeros_like(c), c], axis=-1)


def _na_call(p_ctx, p_lat, rpb, *, col0, n_heads):
    t_ctx, t = p_ctx.shape[0], p_lat.shape[0]
    rows = t // GRID_W
    assert t % GRID_W == 0 and rows % (NA_QROWS * NA_STEP_BLOCKS) == 0 and rows >= NA_KROWS + NA_QROWS
    assert rows % (2 * NA_PREP_TILES) == 0
    assert NA_QROWS % 2 == 0 and NA_KH % 4 == 0
    nq = NA_QROWS * GRID_W
    rtab, ctab = _rope_tables(rows)
    cbias = _na_col_bias(rpb)
    full = lambda col: pl.BlockSpec((t, HEAD_DIM), lambda h: (0, col + h))
    ctx = lambda col: pl.BlockSpec((t_ctx, HEAD_DIM), lambda h: (0, col + h))
    return pl.pallas_call(
        functools.partial(_na_kernel, rows=rows),
        grid=(n_heads,),
        in_specs=[full(col0), full(col0 + n_heads), full(col0 + 2 * n_heads),
                  ctx(col0 + n_heads), ctx(col0 + 2 * n_heads),
                  pl.BlockSpec((2, rows, HEAD_DIM), lambda h: (0, 0, 0)),
                  pl.BlockSpec((2, GRID_W, HEAD_DIM), lambda h: (0, 0, 0)),
                  pl.BlockSpec((None, 2 * NA_KH - 1, GRID_W, GRID_W), lambda h: (h, 0, 0, 0))],
        out_specs=pl.BlockSpec((t, HEAD_DIM), lambda h: (0, h)),
        out_shape=jax.ShapeDtypeStruct((t, n_heads * HEAD_DIM), BF16),
        scratch_shapes=[pltpu.VMEM((rows // 2, HEAD_DIM, 2 * GRID_W), BF16),
                        pltpu.VMEM((rows // 2, 2 * GRID_W, HEAD_DIM), BF16),
                        pltpu.VMEM((HEAD_DIM, t_ctx), BF16),
                        pltpu.VMEM((3, nq, NA_KROWS * GRID_W), F32)],
        compiler_params=_params("parallel"),
        name="natten",
    )(p_lat, p_lat, p_lat, p_ctx, p_ctx, rtab, ctab, cbias)


def _outproj_kernel(hg_ref, na_ref, h_ref, gate_ref, w_ref, o_ref):
    hw = hg_ref.shape[1]
    mix = _dot(hg_ref[...], w_ref[:hw, :]) + _dot(na_ref[...], w_ref[hw:, :])
    o_ref[...] = h_ref[...] + gate_ref[...] * mix


def _outproj_call(hg, na, h, gate, w, tm=512):
    t, d = h.shape
    hw, nw = hg.shape[1], na.shape[1]
    assert t % tm == 0
    return pl.pallas_call(
        _outproj_kernel,
        grid=(t // tm,),
        in_specs=[pl.BlockSpec((tm, hw), lambda i: (i, 0)),
                  pl.BlockSpec((tm, nw), lambda i: (i, 0)),
                  pl.BlockSpec((tm, d), lambda i: (i, 0)),
                  pl.BlockSpec((1, d), lambda i: (0, 0)),
                  pl.BlockSpec((hw + nw, d), lambda i: (0, 0))],
        out_specs=pl.BlockSpec((tm, d), lambda i: (i, 0)),
        out_shape=jax.ShapeDtypeStruct((t, d), F32),
        compiler_params=_params("parallel"),
        name="outproj",
    )(hg, na, h, gate, w)


def kernel(x, c, ctx, c_ctx, w_mod, b_mod, norm_w, ffn1_w_gate, ffn1_w_up, ffn1_w_down, w_in, hg_lb_logits,
           hg_norm_w, na_rpb, w_out, ffn2_w_gate, ffn2_w_up, ffn2_w_down, final_norm_w):
    batch, _, d = x.shape
    depth = w_mod.shape[0]
    assert batch == 1 and depth == 1
    hg_heads = hg_lb_logits.shape[-1] // HEAD_DIM
    na_heads = na_rpb.shape[1]
    bf = lambda w: w.astype(BF16)
    row = lambda v: v.reshape(1, -1)

    cc = jnp.stack([c[0], c_ctx], axis=1)
    mods = _mod_call(cc, w_mod[0], row(b_mod[0]))
    m_l = mods[0].reshape(N_MOD, d)
    m_c = mods[1].reshape(N_MOD, d)
    lb = jnp.cumsum(jax.nn.softmax(hg_lb_logits.astype(F32), axis=1), axis=1)[:, 0]

    ffn1 = (bf(ffn1_w_gate[0]), bf(ffn1_w_up[0]), bf(ffn1_w_down[0]))
    ffn2 = (bf(ffn2_w_gate[0]), bf(ffn2_w_up[0]), bf(ffn2_w_down[0]))
    w_in_b, w_out_b = bf(w_in[0]), bf(w_out[0])

    h_l = _ffn_call(x[0], m_l[0:3], row(norm_w[0, 0]), *ffn1)
    h_c = _ffn_call(ctx[0], m_c[0:3], row(norm_w[0, 0]), *ffn1)
    p_l = _inproj_call(h_l, m_l[3:5], row(norm_w[0, 1]), w_in_b)
    p_c = _inproj_call(h_c, m_c[3:5], row(norm_w[0, 1]), w_in_b)

    st0 = _hg_ctx_call(p_c, lb, n_heads=hg_heads)
    o_fw = _hg_call(p_l, st0[0], lb[0:1], None, None, rev=False, n_heads=hg_heads)
    hg = _hg_call(p_l, st0[1], lb[1:2], o_fw, row(hg_norm_w[0]), rev=True, n_heads=hg_heads)
    na = _na_call(p_c, p_l, na_rpb[0], col0=5 * hg_heads, n_heads=na_heads)

    h_l = _outproj_call(hg, na, h_l, m_l[5:6], w_out_b)
    out = _ffn_call(h_l, m_l[6:9], row(norm_w[0, 2]), *ffn2, final_nw=row(final_norm_w))
    return out[None]
```

```python
import functools

import numpy as np
import jax
import jax.numpy as jnp
from jax import lax
from jax.experimental import pallas as pl
from jax.experimental.pallas import tpu as pltpu

F32 = jnp.float32
BF16 = jnp.bfloat16

EPS = 1e-6
GRID_W = 64
HEAD_DIM = 128
NA_KH = 8
NA_KW = 16
ROPE_BASE = 10000.0
N_MOD = 9
NEG_BIG = -1e30

V7X_VMEM_BYTES = 64 * 1024 * 1024
VMEM_LIMIT = V7X_VMEM_BYTES - 8 * 1024 * 1024

HG_CHUNK = 128
HG_LEVELS = (64, 32, 16, 8, 4, 2, 1)
NA_QROWS = 4
NA_KROWS = NA_QROWS + NA_KH
NA_STEP_BLOCKS = 4
NA_PREP_TILES = 4


def _dot(a, b):
    return jnp.dot(a, b, preferred_element_type=F32)


def _dot_nt(a, b):
    return lax.dot_general(a, b, (((1,), (1,)), ((), ())), preferred_element_type=F32)


def _dot_tn(a, b):
    return lax.dot_general(a, b, (((0,), (0,)), ((), ())), preferred_element_type=F32)


def _sigmoid(x):
    return 0.5 * jnp.tanh(0.5 * x) + 0.5


def _silu(x):
    return x * _sigmoid(x)


def _rmsnorm(x, w):
    return x * lax.rsqrt(jnp.mean(x * x, axis=-1, keepdims=True) + EPS) * w


def _params(*semantics):
    return pltpu.CompilerParams(dimension_semantics=semantics, vmem_limit_bytes=VMEM_LIMIT)


def _mod_kernel(c_ref, w_ref, b_ref, o_ref, sc_scr):
    d, tn = w_ref.shape
    n_vec = c_ref.shape[1]
    sub, lanes = 8, sc_scr.shape[-1]

    @pl.when(pl.program_id(0) == 0)
    def _():
        sc = _silu(c_ref[...])
        for r in range(n_vec):
            sc_scr[r] = jnp.broadcast_to(sc[:, r:r + 1], (d, lanes))

    def body(i, accs):
        rows = pl.ds(pl.multiple_of(i * sub, sub), sub)
        w = w_ref[rows, :]
        return tuple(acc + w * jnp.concatenate([sc_scr[r, rows, :]] * (tn // lanes), axis=1)
                     for r, acc in enumerate(accs))

    zero = jnp.zeros((sub, tn), F32)
    accs = lax.fori_loop(0, d // sub, body, (zero,) * n_vec, unroll=8)
    for r, acc in enumerate(accs):
        o_ref[r:r + 1, :] = jnp.sum(acc, axis=0, keepdims=True) + b_ref[...]


def _mod_call(cc, w_mod, b_mod):
    d, n_vec = cc.shape
    n = w_mod.shape[1]
    tn = next(t for t in (1024, 512, 256, 128) if n % t == 0)
    return pl.pallas_call(
        _mod_kernel,
        grid=(n // tn,),
        in_specs=[pl.BlockSpec((d, n_vec), lambda j: (0, 0)),
                  pl.BlockSpec((d, tn), lambda j: (0, j)),
                  pl.BlockSpec((1, tn), lambda j: (0, j))],
        out_specs=pl.BlockSpec((n_vec, tn), lambda j: (0, j)),
        out_shape=jax.ShapeDtypeStruct((n_vec, n), F32),
        scratch_shapes=[pltpu.VMEM((n_vec, d, HEAD_DIM), F32)],
        compiler_params=_params("arbitrary"),
        name="mod",
    )(cc, w_mod, b_mod)


def _adaln(x, nw, mod_ref):
    return (_rmsnorm(x, nw) * (1.0 + mod_ref[1:2, :]) + mod_ref[0:1, :]).astype(BF16)


FFN_NEXT_STEP = 1


def _ffn_kernel(x_ref, mod_ref, nw_ref, wg_ref, wu_ref, wd_ref, *rest, nf, final_norm):
    if final_norm:
        fnw_ref, o_ref, u_scr, acc_scr = rest
    else:
        o_ref, u_scr, acc_scr = rest
    i, f = pl.program_id(0), pl.program_id(1)
    slot = i % 2

    @pl.when((i == 0) & (f == 0))
    def _():
        u_scr[0] = _adaln(x_ref[...], nw_ref[...], mod_ref)

    @pl.when(f == 0)
    def _():
        acc_scr[...] = jnp.zeros_like(acc_scr)

    def step():
        u = u_scr[slot]
        a = (_silu(_dot(u, wg_ref[...])) * _dot(u, wu_ref[...])).astype(BF16)
        acc_scr[...] += _dot(a, wd_ref[...])

    with_next = (f == FFN_NEXT_STEP) & (i + 1 < pl.num_programs(0))

    @pl.when(with_next)
    def _():
        u_scr[1 - slot] = _adaln(x_ref[...], nw_ref[...], mod_ref)
        step()

    @pl.when(jnp.logical_not(with_next))
    def _():
        step()

    @pl.when(f == nf - 1)
    def _():
        h = x_ref[...] + 0.5 * mod_ref[2:3, :] * acc_scr[...]
        if final_norm:
            h = _rmsnorm(h, fnw_ref[...])
        o_ref[...] = h


def _ffn_call(x, mod3, nw, wg, wu, wd, final_nw=None, tm=512, tf=512):
    t, d = x.shape
    dff = wg.shape[1]
    tm = min(tm, t)
    nf, n_blk = dff // tf, t // tm
    assert t % tm == 0 and dff % tf == 0 and nf > FFN_NEXT_STEP + 1
    final_norm = final_nw is not None
    x_row = lambda i, f: jnp.where(f == FFN_NEXT_STEP, jnp.minimum(i + 1, n_blk - 1), i)
    in_specs = [pl.BlockSpec((tm, d), lambda i, f: (x_row(i, f), 0)),
                pl.BlockSpec((3, d), lambda i, f: (0, 0)),
                pl.BlockSpec((1, d), lambda i, f: (0, 0)),
                pl.BlockSpec((d, tf), lambda i, f: (0, f)),
                pl.BlockSpec((d, tf), lambda i, f: (0, f)),
                pl.BlockSpec((tf, d), lambda i, f: (f, 0))]
    args = [x, mod3, nw, wg, wu, wd]
    if final_norm:
        in_specs.append(pl.BlockSpec((1, d), lambda i, f: (0, 0)))
        args.append(final_nw)
    return pl.pallas_call(
        functools.partial(_ffn_kernel, nf=nf, final_norm=final_norm),
        grid=(n_blk, nf),
        in_specs=in_specs,
        out_specs=pl.BlockSpec((tm, d), lambda i, f: (i, 0)),
        out_shape=jax.ShapeDtypeStruct((t, d), F32),
        scratch_shapes=[pltpu.VMEM((2, tm, d), BF16), pltpu.VMEM((tm, d), F32)],
        compiler_params=_params("arbitrary", "arbitrary"),
        name="ffn",
    )(*args)


def _inproj_kernel(x_ref, mod_ref, nw_ref, w_ref, o_ref, u_scr):
    i, j = pl.program_id(0), pl.program_id(1)
    slot = i % 2

    @pl.when((i == 0) & (j == 0))
    def _():
        u_scr[0] = _adaln(x_ref[...], nw_ref[...], mod_ref)

    with_next = (j == FFN_NEXT_STEP) & (i + 1 < pl.num_programs(0))

    @pl.when(with_next)
    def _():
        o_ref[...] = _dot(u_scr[slot], w_ref[...])
        u_scr[1 - slot] = _adaln(x_ref[...], nw_ref[...], mod_ref)

    @pl.when(jnp.logical_not(with_next))
    def _():
        o_ref[...] = _dot(u_scr[slot], w_ref[...])


def _inproj_call(x, mod2, nw, w, tm=1024, tn=1024):
    t, d = x.shape
    n = w.shape[1]
    tm = min(tm, t)
    n_blk = t // tm
    assert t % tm == 0 and n % tn == 0 and n // tn > FFN_NEXT_STEP
    x_row = lambda i, j: jnp.where((i == 0) & (j == 0), 0, jnp.minimum(i + 1, n_blk - 1))
    return pl.pallas_call(
        _inproj_kernel,
        grid=(n_blk, n // tn),
        in_specs=[pl.BlockSpec((tm, d), lambda i, j: (x_row(i, j), 0)),
                  pl.BlockSpec((2, d), lambda i, j: (0, 0)),
                  pl.BlockSpec((1, d), lambda i, j: (0, 0)),
                  pl.BlockSpec((d, tn), lambda i, j: (0, j))],
        out_specs=pl.BlockSpec((tm, tn), lambda i, j: (i, j)),
        out_shape=jax.ShapeDtypeStruct((t, n), F32),
        scratch_shapes=[pltpu.VMEM((2, tm, d), BF16)],
        compiler_params=_params("arbitrary", "arbitrary"),
        name="inproj",
    )(x, mod2, nw, w)


def _split3(x):
    hi = x.astype(BF16)
    r = x - hi.astype(F32)
    mid = r.astype(BF16)
    lo = (r - mid.astype(F32)).astype(BF16)
    return hi, mid, lo


def _hg_gates(q_raw, f_raw, lb, tri):
    q = _silu(q_raw)
    f = lb + (1.0 - lb) * _sigmoid(f_raw)
    hi, mid, lo = _split3(jnp.log(f))
    return q, 1.0 - f, f, _dot(tri, hi) + _dot(tri, mid) + _dot(tri, lo)


def _hg_level_operand(m, q, k, f, b, b_row, pos, rev):
    c = HG_CHUNK
    if m >= 8:
        qk, bmid = [], []
        for a in range(c // (2 * m)):
            lo_r, mid_r, hi_r = a * 2 * m, a * 2 * m + m, (a + 1) * 2 * m
            qk += [q[lo_r:mid_r], k[mid_r:hi_r]] if rev else [k[lo_r:mid_r], q[mid_r:hi_r]]
            bmid.append(jnp.broadcast_to(b_row(mid_r if rev else mid_r - 1), (2 * m, HEAD_DIM)))
        qk = jnp.concatenate(qk, axis=0)
        bmid = bmid[0] if len(bmid) == 1 else jnp.concatenate(bmid, axis=0)
        return (qk * jnp.exp(-jnp.abs(b - bmid))).astype(BF16)
    half = pos & (2 * m - 1)
    q_side = (half < m) if rev else (half >= m)
    if m == 4:
        rows = [jnp.broadcast_to(b_row(a * 8 + (4 if rev else 3)), (8, HEAD_DIM)) for a in range(c // 8)]
        w = jnp.exp(-jnp.abs(b - jnp.concatenate(rows, axis=0)))
    elif m == 2:
        f_next = pltpu.roll(f, c - 1, 0)
        f_prev = pltpu.roll(f, 1, 0)
        if rev:
            w = jnp.where(half == 0, f * f_next, jnp.where(half == 1, f, jnp.where(half == 2, 1.0, f_prev)))
        else:
            w = jnp.where(half == 0, f_next, jnp.where(half == 1, 1.0, jnp.where(half == 2, f, f * f_prev)))
    else:
        w = jnp.where(q_side, f, 1.0)
    return (jnp.where(q_side, q, k) * w).astype(BF16)


def _hg_state_step(st, k, vb, b, b_end):
    return st * jnp.exp(b_end) + _dot_tn(vb, (k * jnp.exp(b_end - b)).astype(BF16))


def _hg_ctx_kernel(q_ref, ff_ref, fb_ref, v_ref, lb_ref, tri_ref, o_ref, b_scr):
    c = HG_CHUNK
    n_chunks = q_ref.shape[0] // c
    for d, (f_ref, rev) in enumerate(((ff_ref, False), (fb_ref, True))):
        st = jnp.zeros((HEAD_DIM, HEAD_DIM), F32)
        for ci in (range(n_chunks - 1, -1, -1) if rev else range(n_chunks)):
            sl = pl.ds(ci * c, c)
            _, k, _, b = _hg_gates(q_ref[sl, :], f_ref[sl, :], lb_ref[d:d + 1, :], tri_ref[d])
            b_scr[...] = b
            st = _hg_state_step(st, k, v_ref[sl, :].astype(BF16), b, b_scr[pl.ds(0 if rev else c - 1, 1), :])
        o_ref[d] = st


def _hg_kernel(q_ref, f_ref, v_ref, st0_ref, lb_ref, tri_ref, lvl_ref, *rest, rev, heads_per_step, unroll,
               readout):
    if readout:
        of_ref, g_ref, nw_ref, o_ref, b_scr, st_scr = rest
    else:
        o_ref, b_scr, st_scr = rest
    c = HG_CHUNK

    @pl.when(pl.program_id(1) == 0)
    def _():
        st_scr[...] = st0_ref[...]

    n_chunks = q_ref.shape[0] // c
    tri = tri_ref[...]

    def body(i, carry):
        where = []
        for u in range(unroll):
            ci = i * unroll + u
            ci = (n_chunks - 1 - ci) if rev else ci
            rows = pl.ds(pl.multiple_of(ci * c, c), c)
            where += [(rows, slice(h * HEAD_DIM, (h + 1) * HEAD_DIM)) for h in range(heads_per_step)]
        n = len(where)
        gates = [_hg_gates(q_ref[r, cs], f_ref[r, cs], lb_ref[:, cs], tri) for r, cs in where]
        for p in range(n):
            b_scr[p] = gates[p][3]
        vbs = [v_ref[r, cs].astype(BF16) for r, cs in where]

        pos = lax.broadcasted_iota(jnp.int32, (c, HEAD_DIM), 0)
        lvl = lvl_ref[...]
        scores = [0.0] * n
        for m in HG_LEVELS:
            for p in range(n):
                q, k, f, b = gates[p]
                x = _hg_level_operand(m, q, k, f, b, lambda r, p=p: b_scr[p, pl.ds(r, 1), :], pos, rev)
                scores[p] = jnp.where(lvl == m, _dot_nt(x, x), scores[p])

        outs = [None] * n
        for h in range(heads_per_step):
            st = st_scr[h]
            for u in range(unroll):
                p = u * heads_per_step + h
                q, k, f, b = gates[p]
                outs[p] = _dot_nt((q * jnp.exp(b)).astype(BF16), st.astype(BF16))
                st = _hg_state_step(st, k, vbs[p], b, b_scr[p, pl.ds(0 if rev else c - 1, 1), :])
            st_scr[h] = st

        for p, (r, cs) in enumerate(where):
            q, k, f, b = gates[p]
            o = outs[p] + _dot(scores[p].astype(BF16), vbs[p])
            o = o + jnp.sum(q * k, axis=-1, keepdims=True) * v_ref[r, cs]
            if readout:
                o = _rmsnorm(of_ref[r, cs] + o, nw_ref[...]) * _silu(g_ref[r, cs])
            o_ref[r, cs] = o.astype(o_ref.dtype)
        return carry

    lax.fori_loop(0, n_chunks // unroll, body, 0)


def _hg_tables(rev):
    t = np.arange(HG_CHUNK)[:, None]
    s = np.arange(HG_CHUNK)[None, :]
    x = t ^ s
    msb = np.where(x > 0, 2 ** np.floor(np.log2(np.maximum(x, 1))).astype(np.int64), 0)
    lvl = np.where((s > t) if rev else (s < t), msb, 0).astype(np.int32)
    tri = np.tril(np.ones((HG_CHUNK, HG_CHUNK), np.float32))
    return (tri.T if rev else tri), lvl


def _hg_ctx_call(p_ctx, lb, *, n_heads):
    t_ctx = p_ctx.shape[0]
    assert t_ctx % HG_CHUNK == 0
    tri = jnp.asarray(np.stack([_hg_tables(False)[0], _hg_tables(True)[0]]), BF16)
    col = lambda c0: pl.BlockSpec((t_ctx, HEAD_DIM), lambda h: (0, c0 + h))
    const = pl.BlockSpec((2, HG_CHUNK, HG_CHUNK), lambda h: (0, 0, 0))
    return pl.pallas_call(
        _hg_ctx_kernel,
        grid=(n_heads,),
        in_specs=[col(0), col(n_heads), col(2 * n_heads), col(3 * n_heads),
                  pl.BlockSpec((2, HEAD_DIM), lambda h: (0, h)), const],
        out_specs=pl.BlockSpec((2, None, HEAD_DIM, HEAD_DIM), lambda h: (0, h, 0, 0)),
        out_shape=jax.ShapeDtypeStruct((2, n_heads, HEAD_DIM, HEAD_DIM), F32),
        scratch_shapes=[pltpu.VMEM((HG_CHUNK, HEAD_DIM), F32)],
        compiler_params=_params("parallel"),
        name="hgrn_ctx",
    )(p_ctx, p_ctx, p_ctx, p_ctx, lb, tri)


def _hg_call(p_lat, st0, lb_dir, o_fw, norm_w, *, rev, n_heads, tb=1024, heads_per_step=2, unroll=4):
    t_lat = p_lat.shape[0]
    tb = min(tb, t_lat)
    hb = heads_per_step
    assert t_lat % tb == 0 and tb % (HG_CHUNK * unroll) == 0 and n_heads % hb == 0
    n_blk, n_grp = t_lat // tb, n_heads // hb
    readout = o_fw is not None
    blk_row = (lambda j: n_blk - 1 - j) if rev else (lambda j: j)
    lat = lambda c0: pl.BlockSpec((tb, hb * HEAD_DIM), lambda g, j: (blk_row(j), c0 * n_grp + g))
    tri, lvl = _hg_tables(rev)
    const = pl.BlockSpec((HG_CHUNK, HG_CHUNK), lambda g, j: (0, 0))
    in_specs = [lat(0), lat(2 if rev else 1), lat(3),
                pl.BlockSpec((hb, HEAD_DIM, HEAD_DIM), lambda g, j: (g, 0, 0)),
                pl.BlockSpec((1, hb * HEAD_DIM), lambda g, j: (0, g)), const, const]
    args = [p_lat, p_lat, p_lat, st0, lb_dir, jnp.asarray(tri, BF16), jnp.asarray(lvl)]
    out_spec = pl.BlockSpec((tb, hb * HEAD_DIM), lambda g, j: (blk_row(j), g))
    if readout:
        in_specs += [out_spec, lat(4), pl.BlockSpec((1, HEAD_DIM), lambda g, j: (0, 0))]
        args += [o_fw, p_lat, norm_w]
    return pl.pallas_call(
        functools.partial(_hg_kernel, rev=rev, heads_per_step=hb, unroll=unroll, readout=readout),
        grid=(n_grp, n_blk),
        in_specs=in_specs,
        out_specs=out_spec,
        out_shape=jax.ShapeDtypeStruct((t_lat, n_heads * HEAD_DIM), BF16 if readout else F32),
        scratch_shapes=[pltpu.VMEM((unroll * hb, HG_CHUNK, HEAD_DIM), F32),
                        pltpu.VMEM((hb, HEAD_DIM, HEAD_DIM), F32)],
        compiler_params=_params("parallel", "arbitrary"),
        name="hgrn_bwd" if rev else "hgrn_fwd",
    )(*args)


def _rope(x, cos, sin_signed, low):
    return x * cos + jnp.where(low, pltpu.roll(x, 96, 1), pltpu.roll(x, 32, 1)) * sin_signed


def _na_kernel(q_ref, k_ref, v_ref, kc_ref, vc_ref, rtab_ref, ctab_ref, cbias_ref,
               o_ref, kt_scr, v_scr, kct_scr, bias_scr, *, rows):
    tile = 2 * GRID_W
    n_rb = rows // NA_QROWS
    lane = lax.broadcasted_iota(jnp.int32, (GRID_W, HEAD_DIM), 1)
    low = (lane & (HEAD_DIM // 2 - 1)) < HEAD_DIM // 4

    def rope_rows(x_ref, row0, n):
        out = []
        for i in range(n):
            x = x_ref[pl.ds(pl.multiple_of((row0 + i) * GRID_W, GRID_W), GRID_W), :]
            cos = rtab_ref[0, pl.ds(row0 + i, 1), :] + ctab_ref[0]
            sin = rtab_ref[1, pl.ds(row0 + i, 1), :] + ctab_ref[1]
            out.append(_rope(x, cos, sin, low))
        return jnp.concatenate(out, axis=0)

    def prep(i, carry):
        js = [i * NA_PREP_TILES + u for u in range(NA_PREP_TILES)]
        roped = [rope_rows(k_ref, 2 * j, 2) for j in js]
        for j, kr in zip(js, roped):
            kt_scr[j] = kr.T.astype(BF16)
            v_scr[j] = v_ref[pl.ds(pl.multiple_of(j * tile, tile), tile), :].astype(BF16)
        return carry

    lax.fori_loop(0, rows // (2 * NA_PREP_TILES), prep, 0)
    kct_scr[...] = kc_ref[...].T.astype(BF16)

    bias_scr[...] = jnp.full(bias_scr.shape, NEG_BIG, F32)
    for e in range(3):
        for a in range(NA_QROWS):
            b0 = (0, a, NA_KROWS - NA_KH)[e]
            d0 = (NA_KH - 1 - a, NA_KH // 2 - 1, NA_QROWS - 1 - a)[e]
            for i in range(NA_KH):
                bias_scr[e, a * GRID_W:(a + 1) * GRID_W, (b0 + i) * GRID_W:(b0 + i + 1) * GRID_W] = (
                    cbias_ref[d0 + i])

    vc = vc_ref[...].astype(BF16)

    def row_blocks(i, carry):
        rbs = [i * NA_STEP_BLOCKS + u for u in range(NA_STEP_BLOCKS)]
        qs = [(rope_rows(q_ref, rb * NA_QROWS, NA_QROWS) * (HEAD_DIM ** -0.5)).astype(BF16) for rb in rbs]
        j0s = [jnp.clip(rb * (NA_QROWS // 2) - NA_KH // 4, 0, (rows - NA_KROWS) // 2) for rb in rbs]
        s_w, s_c = [], []
        for rb, q, j0 in zip(rbs, qs, j0s):
            kw = jnp.concatenate([kt_scr[j0 + t] for t in range(NA_KROWS // 2)], axis=1)
            edge = jnp.where(rb == 0, 0, jnp.where(rb == n_rb - 1, 2, 1))
            s_w.append(_dot(q, kw) + bias_scr[edge])
            s_c.append(_dot(q, kct_scr[...]))
        nq = NA_QROWS * GRID_W
        for rb, j0, sw, sc in zip(rbs, j0s, s_w, s_c):
            m = jnp.maximum(jnp.max(sw, axis=-1, keepdims=True), jnp.max(sc, axis=-1, keepdims=True))
            p_w = jnp.exp(sw - m)
            p_c = jnp.exp(sc - m)
            denom = jnp.sum(p_w, axis=-1, keepdims=True) + jnp.sum(p_c, axis=-1, keepdims=True)
            vw = jnp.concatenate([v_scr[j0 + t] for t in range(NA_KROWS // 2)], axis=0)
            o = _dot(p_w.astype(BF16), vw) + _dot(p_c.astype(BF16), vc)
            o_ref[pl.ds(pl.multiple_of(rb * nq, nq), nq), :] = (o / denom).astype(o_ref.dtype)
        return carry

    lax.fori_loop(0, n_rb // NA_STEP_BLOCKS, row_blocks, 0)


def _na_col_bias(rpb):
    col = np.arange(GRID_W)
    col_start = np.clip(col - NA_KW // 2, 0, GRID_W - NA_KW)
    col_in = (col[None, :] >= col_start[:, None]) & (col[None, :] < col_start[:, None] + NA_KW)
    dc = np.clip(col[None, :] - col[:, None], 1 - NA_KW, NA_KW - 1) + (NA_KW - 1)
    onehot = (dc[None] == np.arange(2 * NA_KW - 1)[:, None, None]).astype(np.float32)
    vals = jnp.einsum("hrk,kqc->hrqc", rpb.astype(F32), onehot, precision=lax.Precision.HIGHEST)
    return jnp.where(col_in[None, None], vals, NEG_BIG)


def _rope_tables(rows):
    quarter = HEAD_DIM // 4
    inv = ROPE_BASE ** (-jnp.arange(quarter, dtype=F32) / quarter)
    ang_r = jnp.arange(rows).astype(F32)[:, None] * inv[None, :]
    ang_c = jnp.arange(GRID_W).astype(F32)[:, None] * inv[None, :]
    half = lambda a: jnp.stack([jnp.concatenate([jnp.cos(a), jnp.cos(a)], axis=-1),
                                jnp.concatenate([-jnp.sin(a), jnp.sin(a)], axis=-1)])
    r, c = half(ang_r), half(ang_c)
    return jnp.concatenate([r, jnp.zeros_like(r)], axis=-1), jnp.concatenate([jnp.zeros_like(c), c], axis=-1)


def _na_call(p_ctx, p_lat, rpb, *, col0, n_heads):
    t_ctx, t = p_ctx.shape[0], p_lat.shape[0]
    rows = t // GRID_W
    assert t % GRID_W == 0 and rows % (NA_QROWS * NA_STEP_BLOCKS) == 0 and rows >= NA_KROWS + NA_QROWS
    assert rows % (2 * NA_PREP_TILES) == 0
    assert NA_QROWS % 2 == 0 and NA_KH % 4 == 0
    nq = NA_QROWS * GRID_W
    rtab, ctab = _rope_tables(rows)
    cbias = _na_col_bias(rpb)
    full = lambda col: pl.BlockSpec((t, HEAD_DIM), lambda h: (0, col + h))
    ctx = lambda col: pl.BlockSpec((t_ctx, HEAD_DIM), lambda h: (0, col + h))
    return pl.pallas_call(
        functools.partial(_na_kernel, rows=rows),
        grid=(n_heads,),
        in_specs=[full(col0), full(col0 + n_heads), full(col0 + 2 * n_heads),
                  ctx(col0 + n_heads), ctx(col0 + 2 * n_heads),
                  pl.BlockSpec((2, rows, HEAD_DIM), lambda h: (0, 0, 0)),
                  pl.BlockSpec((2, GRID_W, HEAD_DIM), lambda h: (0, 0, 0)),
                  pl.BlockSpec((None, 2 * NA_KH - 1, GRID_W, GRID_W), lambda h: (h, 0, 0, 0))],
        out_specs=pl.BlockSpec((t, HEAD_DIM), lambda h: (0, h)),
        out_shape=jax.ShapeDtypeStruct((t, n_heads * HEAD_DIM), BF16),
        scratch_shapes=[pltpu.VMEM((rows // 2, HEAD_DIM, 2 * GRID_W), BF16),
                        pltpu.VMEM((rows // 2, 2 * GRID_W, HEAD_DIM), BF16),
                        pltpu.VMEM((HEAD_DIM, t_ctx), BF16),
                        pltpu.VMEM((3, nq, NA_KROWS * GRID_W), F32)],
        compiler_params=_params("parallel"),
        name="natten",
    )(p_lat, p_lat, p_lat, p_ctx, p_ctx, rtab, ctab, cbias)


def _outproj_kernel(hg_ref, na_ref, h_ref, gate_ref, w_ref, o_ref):
    hw = hg_ref.shape[1]
    mix = _dot(hg_ref[...], w_ref[:hw, :]) + _dot(na_ref[...], w_ref[hw:, :])
    o_ref[...] = h_ref[...] + gate_ref[...] * mix


def _outproj_call(hg, na, h, gate, w, tm=512):
    t, d = h.shape
    hw, nw = hg.shape[1], na.shape[1]
    assert t % tm == 0
    return pl.pallas_call(
        _outproj_kernel,
        grid=(t // tm,),
        in_specs=[pl.BlockSpec((tm, hw), lambda i: (i, 0)),
                  pl.BlockSpec((tm, nw), lambda i: (i, 0)),
                  pl.BlockSpec((tm, d), lambda i: (i, 0)),
                  pl.BlockSpec((1, d), lambda i: (0, 0)),
                  pl.BlockSpec((hw + nw, d), lambda i: (0, 0))],
        out_specs=pl.BlockSpec((tm, d), lambda i: (i, 0)),
        out_shape=jax.ShapeDtypeStruct((t, d), F32),
        compiler_params=_params("parallel"),
        name="outproj",
    )(hg, na, h, gate, w)


def kernel(x, c, ctx, c_ctx, w_mod, b_mod, norm_w, ffn1_w_gate, ffn1_w_up, ffn1_w_down, w_in, hg_lb_logits,
           hg_norm_w, na_rpb, w_out, ffn2_w_gate, ffn2_w_up, ffn2_w_down, final_norm_w):
    batch, _, d = x.shape
    depth = w_mod.shape[0]
    assert batch == 1 and depth == 1
    hg_heads = hg_lb_logits.shape[-1] // HEAD_DIM
    na_heads = na_rpb.shape[1]
    bf = lambda w: w.astype(BF16)
    row = lambda v: v.reshape(1, -1)

    cc = jnp.stack([c[0], c_ctx], axis=1)
    mods = _mod_call(cc, w_mod[0], row(b_mod[0]))
    m_l = mods[0].reshape(N_MOD, d)
    m_c = mods[1].reshape(N_MOD, d)
    lb = jnp.cumsum(jax.nn.softmax(hg_lb_logits.astype(F32), axis=1), axis=1)[:, 0]

    ffn1 = (bf(ffn1_w_gate[0]), bf(ffn1_w_up[0]), bf(ffn1_w_down[0]))
    ffn2 = (bf(ffn2_w_gate[0]), bf(ffn2_w_up[0]), bf(ffn2_w_down[0]))
    w_in_b, w_out_b = bf(w_in[0]), bf(w_out[0])

    h_l = _ffn_call(x[0], m_l[0:3], row(norm_w[0, 0]), *ffn1)
    h_c = _ffn_call(ctx[0], m_c[0:3], row(norm_w[0, 0]), *ffn1)
    p_l = _inproj_call(h_l, m_l[3:5], row(norm_w[0, 1]), w_in_b)
    p_c = _inproj_call(h_c, m_c[3:5], row(norm_w[0, 1]), w_in_b)

    st0 = _hg_ctx_call(p_c, lb, n_heads=hg_heads)
    o_fw = _hg_call(p_l, st0[0], lb[0:1], None, None, rev=False, n_heads=hg_heads)
    hg = _hg_call(p_l, st0[1], lb[1:2], o_fw, row(hg_norm_w[0]), rev=True, n_heads=hg_heads)
    na = _na_call(p_c, p_l, na_rpb[0], col0=5 * hg_heads, n_heads=na_heads)

    h_l = _outproj_call(hg, na, h_l, m_l[5:6], w_out_b)
    out = _ffn_call(h_l, m_l[6:9], row(norm_w[0, 2]), *ffn2, final_nw=row(final_norm_w))
    return out[None]
```

```python
import functools

import numpy as np
import jax
import jax.numpy as jnp
from jax import lax
from jax.experimental import pallas as pl
from jax.experimental.pallas import tpu as pltpu

F32 = jnp.float32
BF16 = jnp.bfloat16

EPS = 1e-6
GRID_W = 64
HEAD_DIM = 128
NA_KH = 8
NA_KW = 16
ROPE_BASE = 10000.0
N_MOD = 9
NEG_BIG = -1e30

V7X_VMEM_BYTES = 64 * 1024 * 1024
VMEM_LIMIT = V7X_VMEM_BYTES - 8 * 1024 * 1024

HG_CHUNK = 128
HG_LEVELS = (64, 32, 16, 8, 4, 2, 1)
NA_QROWS = 4
NA_KROWS = NA_QROWS + NA_KH
NA_STEP_BLOCKS = 4
NA_PREP_TILES = 4


def _dot(a, b):
    return jnp.dot(a, b, preferred_element_type=F32)


def _dot_nt(a, b):
    return lax.dot_general(a, b, (((1,), (1,)), ((), ())), preferred_element_type=F32)


def _dot_tn(a, b):
    return lax.dot_general(a, b, (((0,), (0,)), ((), ())), preferred_element_type=F32)


def _sigmoid(x):
    return 0.5 * jnp.tanh(0.5 * x) + 0.5


def _silu(x):
    return x * _sigmoid(x)


def _rmsnorm(x, w):
    return x * lax.rsqrt(jnp.mean(x * x, axis=-1, keepdims=True) + EPS) * w


def _params(*semantics):
    return pltpu.CompilerParams(dimension_semantics=semantics, vmem_limit_bytes=VMEM_LIMIT)


def _mod_kernel(c_ref, w_ref, b_ref, o_ref, sc_scr):
    d, tn = w_ref.shape
    n_vec = c_ref.shape[1]
    sub, lanes = 8, sc_scr.shape[-1]

    @pl.when(pl.program_id(0) == 0)
    def _():
        sc = _silu(c_ref[...])
        for r in range(n_vec):
            sc_scr[r] = jnp.broadcast_to(sc[:, r:r + 1], (d, lanes))

    def body(i, accs):
        rows = pl.ds(pl.multiple_of(i * sub, sub), sub)
        w = w_ref[rows, :]
        return tuple(acc + w * jnp.concatenate([sc_scr[r, rows, :]] * (tn // lanes), axis=1)
                     for r, acc in enumerate(accs))

    zero = jnp.zeros((sub, tn), F32)
    accs = lax.fori_loop(0, d // sub, body, (zero,) * n_vec, unroll=8)
    for r, acc in enumerate(accs):
        o_ref[r:r + 1, :] = jnp.sum(acc, axis=0, keepdims=True) + b_ref[...]


def _mod_call(cc, w_mod, b_mod):
    d, n_vec = cc.shape
    n = w_mod.shape[1]
    tn = next(t for t in (1024, 512, 256, 128) if n % t == 0)
    return pl.pallas_call(
        _mod_kernel,
        grid=(n // tn,),
        in_specs=[pl.BlockSpec((d, n_vec), lambda j: (0, 0)),
                  pl.BlockSpec((d, tn), lambda j: (0, j)),
                  pl.BlockSpec((1, tn), lambda j: (0, j))],
        out_specs=pl.BlockSpec((n_vec, tn), lambda j: (0, j)),
        out_shape=jax.ShapeDtypeStruct((n_vec, n), F32),
        scratch_shapes=[pltpu.VMEM((n_vec, d, HEAD_DIM), F32)],
        compiler_params=_params("arbitrary"),
        name="mod",
    )(cc, w_mod, b_mod)


def _adaln(x, nw, mod_ref):
    return (_rmsnorm(x, nw) * (1.0 + mod_ref[1:2, :]) + mod_ref[0:1, :]).astype(BF16)


FFN_NEXT_STEP = 1


def _ffn_kernel(x_ref, mod_ref, nw_ref, wg_ref, wu_ref, wd_ref, *rest, nf, final_norm):
    if final_norm:
        fnw_ref, o_ref, u_scr, x_scr, acc_scr = rest
    else:
        o_ref, u_scr, x_scr, acc_scr = rest
    i, f = pl.program_id(0), pl.program_id(1)
    slot = i % 2

    def take_block(dst):
        x = x_ref[...]
        x_scr[dst] = x
        u_scr[dst] = _adaln(x, nw_ref[...], mod_ref)

    @pl.when((i == 0) & (f == 0))
    def _():
        take_block(0)

    def tile_product():
        u = u_scr[slot]
        a = (_silu(_dot(u, wg_ref[...])) * _dot(u, wu_ref[...])).astype(BF16)
        return _dot(a, wd_ref[...])

    with_next = (f == FFN_NEXT_STEP) & (i + 1 < pl.num_programs(0))

    @pl.when(f == 0)
    def _():
        acc_scr[...] = tile_product()

    @pl.when(with_next)
    def _():
        take_block(1 - slot)
        acc_scr[...] += tile_product()

    @pl.when((f > 0) & (f < nf - 1) & jnp.logical_not(with_next))
    def _():
        acc_scr[...] += tile_product()

    @pl.when(f == nf - 1)
    def _():
        h = x_scr[slot] + 0.5 * mod_ref[2:3, :] * (acc_scr[...] + tile_product())
        if final_norm:
            h = _rmsnorm(h, fnw_ref[...])
        o_ref[...] = h


def _ffn_call(x, mod3, nw, wg, wu, wd, final_nw=None, tm=512, tf=512):
    t, d = x.shape
    dff = wg.shape[1]
    tm = min(tm, t)
    nf, n_blk = dff // tf, t // tm
    assert t % tm == 0 and dff % tf == 0 and nf > FFN_NEXT_STEP + 1
    final_norm = final_nw is not None
    x_row = lambda i, f: jnp.where((i == 0) & (f == 0), 0, jnp.minimum(i + 1, n_blk - 1))
    in_specs = [pl.BlockSpec((tm, d), lambda i, f: (x_row(i, f), 0)),
                pl.BlockSpec((3, d), lambda i, f: (0, 0)),
                pl.BlockSpec((1, d), lambda i, f: (0, 0)),
                pl.BlockSpec((d, tf), lambda i, f: (0, f)),
                pl.BlockSpec((d, tf), lambda i, f: (0, f)),
                pl.BlockSpec((tf, d), lambda i, f: (f, 0))]
    args = [x, mod3, nw, wg, wu, wd]
    if final_norm:
        in_specs.append(pl.BlockSpec((1, d), lambda i, f: (0, 0)))
        args.append(final_nw)
    return pl.pallas_call(
        functools.partial(_ffn_kernel, nf=nf, final_norm=final_norm),
        grid=(n_blk, nf),
        in_specs=in_specs,
        out_specs=pl.BlockSpec((tm, d), lambda i, f: (i, 0)),
        out_shape=jax.ShapeDtypeStruct((t, d), F32),
        scratch_shapes=[pltpu.VMEM((2, tm, d), BF16), pltpu.VMEM((2, tm, d), F32), pltpu.VMEM((tm, d), F32)],
        compiler_params=_params("arbitrary", "arbitrary"),
        name="ffn",
    )(*args)


def _inproj_kernel(x_ref, mod_ref, nw_ref, w_ref, o_ref, u_scr):
    i, j = pl.program_id(0), pl.program_id(1)
    slot = i % 2

    @pl.when((i == 0) & (j == 0))
    def _():
        u_scr[0] = _adaln(x_ref[...], nw_ref[...], mod_ref)

    with_next = (j == FFN_NEXT_STEP) & (i + 1 < pl.num_programs(0))

    @pl.when(with_next)
    def _():
        o_ref[...] = _dot(u_scr[slot], w_ref[...])
        u_scr[1 - slot] = _adaln(x_ref[...], nw_ref[...], mod_ref)

    @pl.when(jnp.logical_not(with_next))
    def _():
        o_ref[...] = _dot(u_scr[slot], w_ref[...])


def _inproj_call(x, mod2, nw, w, tm=1024, tn=1024):
    t, d = x.shape
    n = w.shape[1]
    tm = min(tm, t)
    n_blk = t // tm
    assert t % tm == 0 and n % tn == 0 and n // tn > FFN_NEXT_STEP
    x_row = lambda i, j: jnp.where((i == 0) & (j == 0), 0, jnp.minimum(i + 1, n_blk - 1))
    return pl.pallas_call(
        _inproj_kernel,
        grid=(n_blk, n // tn),
        in_specs=[pl.BlockSpec((tm, d), lambda i, j: (x_row(i, j), 0)),
                  pl.BlockSpec((2, d), lambda i, j: (0, 0)),
                  pl.BlockSpec((1, d), lambda i, j: (0, 0)),
                  pl.BlockSpec((d, tn), lambda i, j: (0, j))],
        out_specs=pl.BlockSpec((tm, tn), lambda i, j: (i, j)),
        out_shape=jax.ShapeDtypeStruct((t, n), F32),
        scratch_shapes=[pltpu.VMEM((2, tm, d), BF16)],
        compiler_params=_params("arbitrary", "arbitrary"),
        name="inproj",
    )(x, mod2, nw, w)


def _split3(x):
    hi = x.astype(BF16)
    r = x - hi.astype(F32)
    mid = r.astype(BF16)
    lo = (r - mid.astype(F32)).astype(BF16)
    return hi, mid, lo


def _hg_gates(q_raw, f_raw, lb, tri):
    q = _silu(q_raw)
    f = lb + (1.0 - lb) * _sigmoid(f_raw)
    hi, mid, lo = _split3(jnp.log(f))
    return q, 1.0 - f, f, _dot(tri, hi) + _dot(tri, mid) + _dot(tri, lo)


def _hg_level_operand(m, q, k, f, b, b_row, pos, rev):
    c = HG_CHUNK
    if m >= 8:
        qk, bmid = [], []
        for a in range(c // (2 * m)):
            lo_r, mid_r, hi_r = a * 2 * m, a * 2 * m + m, (a + 1) * 2 * m
            qk += [q[lo_r:mid_r], k[mid_r:hi_r]] if rev else [k[lo_r:mid_r], q[mid_r:hi_r]]
            bmid.append(jnp.broadcast_to(b_row(mid_r if rev else mid_r - 1), (2 * m, HEAD_DIM)))
        qk = jnp.concatenate(qk, axis=0)
        bmid = bmid[0] if len(bmid) == 1 else jnp.concatenate(bmid, axis=0)
        return (qk * jnp.exp(-jnp.abs(b - bmid))).astype(BF16)
    half = pos & (2 * m - 1)
    q_side = (half < m) if rev else (half >= m)
    if m == 4:
        rows = [jnp.broadcast_to(b_row(a * 8 + (4 if rev else 3)), (8, HEAD_DIM)) for a in range(c // 8)]
        w = jnp.exp(-jnp.abs(b - jnp.concatenate(rows, axis=0)))
    elif m == 2:
        f_next = pltpu.roll(f, c - 1, 0)
        f_prev = pltpu.roll(f, 1, 0)
        if rev:
            w = jnp.where(half == 0, f * f_next, jnp.where(half == 1, f, jnp.where(half == 2, 1.0, f_prev)))
        else:
            w = jnp.where(half == 0, f_next, jnp.where(half == 1, 1.0, jnp.where(half == 2, f, f * f_prev)))
    else:
        w = jnp.where(q_side, f, 1.0)
    return (jnp.where(q_side, q, k) * w).astype(BF16)


def _hg_state_step(st, k, vb, b, b_end):
    return st * jnp.exp(b_end) + _dot_tn(vb, (k * jnp.exp(b_end - b)).astype(BF16))


def _hg_ctx_kernel(q_ref, ff_ref, fb_ref, v_ref, lb_ref, tri_ref, o_ref, b_scr):
    c = HG_CHUNK
    n_chunks = q_ref.shape[0] // c
    for d, (f_ref, rev) in enumerate(((ff_ref, False), (fb_ref, True))):
        st = jnp.zeros((HEAD_DIM, HEAD_DIM), F32)
        for ci in (range(n_chunks - 1, -1, -1) if rev else range(n_chunks)):
            sl = pl.ds(ci * c, c)
            _, k, _, b = _hg_gates(q_ref[sl, :], f_ref[sl, :], lb_ref[d:d + 1, :], tri_ref[d])
            b_scr[...] = b
            st = _hg_state_step(st, k, v_ref[sl, :].astype(BF16), b, b_scr[pl.ds(0 if rev else c - 1, 1), :])
        o_ref[d] = st


def _hg_kernel(q_ref, f_ref, v_ref, st0_ref, lb_ref, tri_ref, lvl_ref, *rest, rev, heads_per_step, unroll,
               readout):
    if readout:
        of_ref, g_ref, nw_ref, o_ref, b_scr, st_scr = rest
    else:
        o_ref, b_scr, st_scr = rest
    c = HG_CHUNK

    @pl.when(pl.program_id(1) == 0)
    def _():
        st_scr[...] = st0_ref[...]

    n_chunks = q_ref.shape[0] // c
    tri = tri_ref[...]

    def body(i, carry):
        where = []
        for u in range(unroll):
            ci = i * unroll + u
            ci = (n_chunks - 1 - ci) if rev else ci
            rows = pl.ds(pl.multiple_of(ci * c, c), c)
            where += [(rows, slice(h * HEAD_DIM, (h + 1) * HEAD_DIM)) for h in range(heads_per_step)]
        n = len(where)
        gates = [_hg_gates(q_ref[r, cs], f_ref[r, cs], lb_ref[:, cs], tri) for r, cs in where]
        for p in range(n):
            b_scr[p] = gates[p][3]
        vbs = [v_ref[r, cs].astype(BF16) for r, cs in where]

        pos = lax.broadcasted_iota(jnp.int32, (c, HEAD_DIM), 0)
        lvl = lvl_ref[...]
        scores = [0.0] * n
        for m in HG_LEVELS:
            for p in range(n):
                q, k, f, b = gates[p]
                x = _hg_level_operand(m, q, k, f, b, lambda r, p=p: b_scr[p, pl.ds(r, 1), :], pos, rev)
                scores[p] = jnp.where(lvl == m, _dot_nt(x, x), scores[p])

        outs = [None] * n
        for h in range(heads_per_step):
            st = st_scr[h]
            for u in range(unroll):
                p = u * heads_per_step + h
                q, k, f, b = gates[p]
                outs[p] = _dot_nt((q * jnp.exp(b)).astype(BF16), st.astype(BF16))
                st = _hg_state_step(st, k, vbs[p], b, b_scr[p, pl.ds(0 if rev else c - 1, 1), :])
            st_scr[h] = st

        for p, (r, cs) in enumerate(where):
            q, k, f, b = gates[p]
            o = outs[p] + _dot(scores[p].astype(BF16), vbs[p])
            o = o + jnp.sum(q * k, axis=-1, keepdims=True) * v_ref[r, cs]
            if readout:
                o = _rmsnorm(of_ref[r, cs] + o, nw_ref[...]) * _silu(g_ref[r, cs])
            o_ref[r, cs] = o.astype(o_ref.dtype)
        return carry

    lax.fori_loop(0, n_chunks // unroll, body, 0)


def _hg_tables(rev):
    t = np.arange(HG_CHUNK)[:, None]
    s = np.arange(HG_CHUNK)[None, :]
    x = t ^ s
    msb = np.where(x > 0, 2 ** np.floor(np.log2(np.maximum(x, 1))).astype(np.int64), 0)
    lvl = np.where((s > t) if rev else (s < t), msb, 0).astype(np.int32)
    tri = np.tril(np.ones((HG_CHUNK, HG_CHUNK), np.float32))
    return (tri.T if rev else tri), lvl


def _hg_ctx_call(p_ctx, lb, *, n_heads):
    t_ctx = p_ctx.shape[0]
    assert t_ctx % HG_CHUNK == 0
    tri = jnp.asarray(np.stack([_hg_tables(False)[0], _hg_tables(True)[0]]), BF16)
    col = lambda c0: pl.BlockSpec((t_ctx, HEAD_DIM), lambda h: (0, c0 + h))
    const = pl.BlockSpec((2, HG_CHUNK, HG_CHUNK), lambda h: (0, 0, 0))
    return pl.pallas_call(
        _hg_ctx_kernel,
        grid=(n_heads,),
        in_specs=[col(0), col(n_heads), col(2 * n_heads), col(3 * n_heads),
                  pl.BlockSpec((2, HEAD_DIM), lambda h: (0, h)), const],
        out_specs=pl.BlockSpec((2, None, HEAD_DIM, HEAD_DIM), lambda h: (0, h, 0, 0)),
        out_shape=jax.ShapeDtypeStruct((2, n_heads, HEAD_DIM, HEAD_DIM), F32),
        scratch_shapes=[pltpu.VMEM((HG_CHUNK, HEAD_DIM), F32)],
        compiler_params=_params("parallel"),
        name="hgrn_ctx",
    )(p_ctx, p_ctx, p_ctx, p_ctx, lb, tri)


def _hg_call(p_lat, st0, lb_dir, o_fw, norm_w, *, rev, n_heads, tb=1024, heads_per_step=2, unroll=4):
    t_lat = p_lat.shape[0]
    tb = min(tb, t_lat)
    hb = heads_per_step
    assert t_lat % tb == 0 and tb % (HG_CHUNK * unroll) == 0 and n_heads % hb == 0
    n_blk, n_grp = t_lat // tb, n_heads // hb
    readout = o_fw is not None
    blk_row = (lambda j: n_blk - 1 - j) if rev else (lambda j: j)
    lat = lambda c0: pl.BlockSpec((tb, hb * HEAD_DIM), lambda g, j: (blk_row(j), c0 * n_grp + g))
    tri, lvl = _hg_tables(rev)
    const = pl.BlockSpec((HG_CHUNK, HG_CHUNK), lambda g, j: (0, 0))
    in_specs = [lat(0), lat(2 if rev else 1), lat(3),
                pl.BlockSpec((hb, HEAD_DIM, HEAD_DIM), lambda g, j: (g, 0, 0)),
                pl.BlockSpec((1, hb * HEAD_DIM), lambda g, j: (0, g)), const, const]
    args = [p_lat, p_lat, p_lat, st0, lb_dir, jnp.asarray(tri, BF16), jnp.asarray(lvl)]
    out_spec = pl.BlockSpec((tb, hb * HEAD_DIM), lambda g, j: (blk_row(j), g))
    if readout:
        in_specs += [out_spec, lat(4), pl.BlockSpec((1, HEAD_DIM), lambda g, j: (0, 0))]
        args += [o_fw, p_lat, norm_w]
    return pl.pallas_call(
        functools.partial(_hg_kernel, rev=rev, heads_per_step=hb, unroll=unroll, readout=readout),
        grid=(n_grp, n_blk),
        in_specs=in_specs,
        out_specs=out_spec,
        out_shape=jax.ShapeDtypeStruct((t_lat, n_heads * HEAD_DIM), BF16 if readout else F32),
        scratch_shapes=[pltpu.VMEM((unroll * hb, HG_CHUNK, HEAD_DIM), F32),
                        pltpu.VMEM((hb, HEAD_DIM, HEAD_DIM), F32)],
        compiler_params=_params("parallel", "arbitrary"),
        name="hgrn_bwd" if rev else "hgrn_fwd",
    )(*args)


def _rope(x, cos, sin_signed, low):
    return x * cos + jnp.where(low, pltpu.roll(x, 96, 1), pltpu.roll(x, 32, 1)) * sin_signed


def _na_kernel(q_ref, k_ref, v_ref, kc_ref, vc_ref, rtab_ref, ctab_ref, cbias_ref,
               o_ref, kt_scr, v_scr, kct_scr, bias_scr, *, rows):
    tile = 2 * GRID_W
    n_rb = rows // NA_QROWS
    lane = lax.broadcasted_iota(jnp.int32, (GRID_W, HEAD_DIM), 1)
    low = (lane & (HEAD_DIM // 2 - 1)) < HEAD_DIM // 4

    def rope_rows(x_ref, row0, n):
        out = []
        for i in range(n):
            x = x_ref[pl.ds(pl.multiple_of((row0 + i) * GRID_W, GRID_W), GRID_W), :]
            cos = rtab_ref[0, pl.ds(row0 + i, 1), :] + ctab_ref[0]
            sin = rtab_ref[1, pl.ds(row0 + i, 1), :] + ctab_ref[1]
            out.append(_rope(x, cos, sin, low))
        return jnp.concatenate(out, axis=0)

    def prep(i, carry):
        js = [i * NA_PREP_TILES + u for u in range(NA_PREP_TILES)]
        roped = [rope_rows(k_ref, 2 * j, 2) for j in js]
        for j, kr in zip(js, roped):
            kt_scr[j] = kr.T.astype(BF16)
            v_scr[j] = v_ref[pl.ds(pl.multiple_of(j * tile, tile), tile), :].astype(BF16)
        return carry

    lax.fori_loop(0, rows // (2 * NA_PREP_TILES), prep, 0)
    kct_scr[...] = kc_ref[...].T.astype(BF16)

    bias_scr[...] = jnp.full(bias_scr.shape, NEG_BIG, F32)
    for e in range(3):
        for a in range(NA_QROWS):
            b0 = (0, a, NA_KROWS - NA_KH)[e]
            d0 = (NA_KH - 1 - a, NA_KH // 2 - 1, NA_QROWS - 1 - a)[e]
            for i in range(NA_KH):
                bias_scr[e, a * GRID_W:(a + 1) * GRID_W, (b0 + i) * GRID_W:(b0 + i + 1) * GRID_W] = (
                    cbias_ref[d0 + i])

    vc = vc_ref[...].astype(BF16)

    def row_blocks(i, carry):
        rbs = [i * NA_STEP_BLOCKS + u for u in range(NA_STEP_BLOCKS)]
        qs = [(rope_rows(q_ref, rb * NA_QROWS, NA_QROWS) * (HEAD_DIM ** -0.5)).astype(BF16) for rb in rbs]
        j0s = [jnp.clip(rb * (NA_QROWS // 2) - NA_KH // 4, 0, (rows - NA_KROWS) // 2) for rb in rbs]
        s_w, s_c = [], []
        for rb, q, j0 in zip(rbs, qs, j0s):
            kw = jnp.concatenate([kt_scr[j0 + t] for t in range(NA_KROWS // 2)], axis=1)
            edge = jnp.where(rb == 0, 0, jnp.where(rb == n_rb - 1, 2, 1))
            s_w.append(_dot(q, kw) + bias_scr[edge])
            s_c.append(_dot(q, kct_scr[...]))
        nq = NA_QROWS * GRID_W
        for rb, j0, sw, sc in zip(rbs, j0s, s_w, s_c):
            m = jnp.maximum(jnp.max(sw, axis=-1, keepdims=True), jnp.max(sc, axis=-1, keepdims=True))
            p_w = jnp.exp(sw - m)
            p_c = jnp.exp(sc - m)
            denom = jnp.sum(p_w, axis=-1, keepdims=True) + jnp.sum(p_c, axis=-1, keepdims=True)
            vw = jnp.concatenate([v_scr[j0 + t] for t in range(NA_KROWS // 2)], axis=0)
            o = _dot(p_w.astype(BF16), vw) + _dot(p_c.astype(BF16), vc)
            o_ref[pl.ds(pl.multiple_of(rb * nq, nq), nq), :] = (o / denom).astype(o_ref.dtype)
        return carry

    lax.fori_loop(0, n_rb // NA_STEP_BLOCKS, row_blocks, 0)


def _na_col_bias(rpb):
    col = np.arange(GRID_W)
    col_start = np.clip(col - NA_KW // 2, 0, GRID_W - NA_KW)
    col_in = (col[None, :] >= col_start[:, None]) & (col[None, :] < col_start[:, None] + NA_KW)
    dc = np.clip(col[None, :] - col[:, None], 1 - NA_KW, NA_KW - 1) + (NA_KW - 1)
    onehot = (dc[None] == np.arange(2 * NA_KW - 1)[:, None, None]).astype(np.float32)
    vals = jnp.einsum("hrk,kqc->hrqc", rpb.astype(F32), onehot, precision=lax.Precision.HIGHEST)
    return jnp.where(col_in[None, None], vals, NEG_BIG)


def _rope_tables(rows):
    quarter = HEAD_DIM // 4
    inv = ROPE_BASE ** (-jnp.arange(quarter, dtype=F32) / quarter)
    ang_r = jnp.arange(rows).astype(F32)[:, None] * inv[None, :]
    ang_c = jnp.arange(GRID_W).astype(F32)[:, None] * inv[None, :]
    half = lambda a: jnp.stack([jnp.concatenate([jnp.cos(a), jnp.cos(a)], axis=-1),
                                jnp.concatenate([-jnp.sin(a), jnp.sin(a)], axis=-1)])
    r, c = half(ang_r), half(ang_c)
    return jnp.concatenate([r, jnp.zeros_like(r)], axis=-1), jnp.concatenate([jnp.zeros_like(c), c], axis=-1)


def _na_call(p_ctx, p_lat, rpb, *, col0, n_heads):
    t_ctx, t = p_ctx.shape[0], p_lat.shape[0]
    rows = t // GRID_W
    assert t % GRID_W == 0 and rows % (NA_QROWS * NA_STEP_BLOCKS) == 0 and rows >= NA_KROWS + NA_QROWS
    assert rows % (2 * NA_PREP_TILES) == 0
    assert NA_QROWS % 2 == 0 and NA_KH % 4 == 0
    nq = NA_QROWS * GRID_W
    rtab, ctab = _rope_tables(rows)
    cbias = _na_col_bias(rpb)
    full = lambda col: pl.BlockSpec((t, HEAD_DIM), lambda h: (0, col + h))
    ctx = lambda col: pl.BlockSpec((t_ctx, HEAD_DIM), lambda h: (0, col + h))
    return pl.pallas_call(
        functools.partial(_na_kernel, rows=rows),
        grid=(n_heads,),
        in_specs=[full(col0), full(col0 + n_heads), full(col0 + 2 * n_heads),
                  ctx(col0 + n_heads), ctx(col0 + 2 * n_heads),
                  pl.BlockSpec((2, rows, HEAD_DIM), lambda h: (0, 0, 0)),
                  pl.BlockSpec((2, GRID_W, HEAD_DIM), lambda h: (0, 0, 0)),
                  pl.BlockSpec((None, 2 * NA_KH - 1, GRID_W, GRID_W), lambda h: (h, 0, 0, 0))],
        out_specs=pl.BlockSpec((t, HEAD_DIM), lambda h: (0, h)),
        out_shape=jax.ShapeDtypeStruct((t, n_heads * HEAD_DIM), BF16),
        scratch_shapes=[pltpu.VMEM((rows // 2, HEAD_DIM, 2 * GRID_W), BF16),
                        pltpu.VMEM((rows // 2, 2 * GRID_W, HEAD_DIM), BF16),
                        pltpu.VMEM((HEAD_DIM, t_ctx), BF16),
                        pltpu.VMEM((3, nq, NA_KROWS * GRID_W), F32)],
        compiler_params=_params("parallel"),
        name="natten",
    )(p_lat, p_lat, p_lat, p_ctx, p_ctx, rtab, ctab, cbias)


def _outproj_kernel(hg_ref, na_ref, h_ref, gate_ref, w_ref, o_ref):
    hw = hg_ref.shape[1]
    mix = _dot(hg_ref[...], w_ref[:hw, :]) + _dot(na_ref[...], w_ref[hw:, :])
    o_ref[...] = h_ref[...] + gate_ref[...] * mix


def _outproj_call(hg, na, h, gate, w, tm=512):
    t, d = h.shape
    hw, nw = hg.shape[1], na.shape[1]
    assert t % tm == 0
    return pl.pallas_call(
        _outproj_kernel,
        grid=(t // tm,),
        in_specs=[pl.BlockSpec((tm, hw), lambda i: (i, 0)),
                  pl.BlockSpec((tm, nw), lambda i: (i, 0)),
                  pl.BlockSpec((tm, d), lambda i: (i, 0)),
                  pl.BlockSpec((1, d), lambda i: (0, 0)),
                  pl.BlockSpec((hw + nw, d), lambda i: (0, 0))],
        out_specs=pl.BlockSpec((tm, d), lambda i: (i, 0)),
        out_shape=jax.ShapeDtypeStruct((t, d), F32),
        compiler_params=_params("parallel"),
        name="outproj",
    )(hg, na, h, gate, w)


def kernel(x, c, ctx, c_ctx, w_mod, b_mod, norm_w, ffn1_w_gate, ffn1_w_up, ffn1_w_down, w_in, hg_lb_logits,
           hg_norm_w, na_rpb, w_out, ffn2_w_gate, ffn2_w_up, ffn2_w_down, final_norm_w):
    batch, _, d = x.shape
    depth = w_mod.shape[0]
    assert batch == 1 and depth == 1
    hg_heads = hg_lb_logits.shape[-1] // HEAD_DIM
    na_heads = na_rpb.shape[1]
    bf = lambda w: w.astype(BF16)
    row = lambda v: v.reshape(1, -1)

    cc = jnp.stack([c[0], c_ctx], axis=1)
    mods = _mod_call(cc, w_mod[0], row(b_mod[0]))
    m_l = mods[0].reshape(N_MOD, d)
    m_c = mods[1].reshape(N_MOD, d)
    lb = jnp.cumsum(jax.nn.softmax(hg_lb_logits.astype(F32), axis=1), axis=1)[:, 0]

    ffn1 = (bf(ffn1_w_gate[0]), bf(ffn1_w_up[0]), bf(ffn1_w_down[0]))
    ffn2 = (bf(ffn2_w_gate[0]), bf(ffn2_w_up[0]), bf(ffn2_w_down[0]))
    w_in_b, w_out_b = bf(w_in[0]), bf(w_out[0])

    h_l = _ffn_call(x[0], m_l[0:3], row(norm_w[0, 0]), *ffn1)
    h_c = _ffn_call(ctx[0], m_c[0:3], row(norm_w[0, 0]), *ffn1)
    p_l = _inproj_call(h_l, m_l[3:5], row(norm_w[0, 1]), w_in_b)
    p_c = _inproj_call(h_c, m_c[3:5], row(norm_w[0, 1]), w_in_b)

    st0 = _hg_ctx_call(p_c, lb, n_heads=hg_heads)
    o_fw = _hg_call(p_l, st0[0], lb[0:1], None, None, rev=False, n_heads=hg_heads)
    hg = _hg_call(p_l, st0[1], lb[1:2], o_fw, row(hg_norm_w[0]), rev=True, n_heads=hg_heads)
    na = _na_call(p_c, p_l, na_rpb[0], col0=5 * hg_heads, n_heads=na_heads)

    h_l = _outproj_call(hg, na, h_l, m_l[5:6], w_out_b)
    out = _ffn_call(h_l, m_l[6:9], row(norm_w[0, 2]), *ffn2, final_nw=row(final_norm_w))
    return out[None]
```

```python
import functools

import numpy as np
import jax
import jax.numpy as jnp
from jax import lax
from jax.experimental import pallas as pl
from jax.experimental.pallas import tpu as pltpu

F32 = jnp.float32
BF16 = jnp.bfloat16

EPS = 1e-6
GRID_W = 64
HEAD_DIM = 128
NA_KH = 8
NA_KW = 16
ROPE_BASE = 10000.0
N_MOD = 9
NEG_BIG = -1e30

V7X_VMEM_BYTES = 64 * 1024 * 1024
VMEM_LIMIT = V7X_VMEM_BYTES - 8 * 1024 * 1024

HG_CHUNK = 128
HG_LEVELS = (64, 32, 16, 8, 4, 2, 1)
NA_QROWS = 4
NA_KROWS = NA_QROWS + NA_KH
NA_STEP_BLOCKS = 4
NA_PREP_TILES = 4


def _dot(a, b):
    return jnp.dot(a, b, preferred_element_type=F32)


def _dot_nt(a, b):
    return lax.dot_general(a, b, (((1,), (1,)), ((), ())), preferred_element_type=F32)


def _dot_tn(a, b):
    return lax.dot_general(a, b, (((0,), (0,)), ((), ())), preferred_element_type=F32)


def _sigmoid(x):
    return 0.5 * jnp.tanh(0.5 * x) + 0.5


def _silu(x):
    return x * _sigmoid(x)


def _rmsnorm(x, w):
    return x * lax.rsqrt(jnp.mean(x * x, axis=-1, keepdims=True) + EPS) * w


def _params(*semantics):
    return pltpu.CompilerParams(dimension_semantics=semantics, vmem_limit_bytes=VMEM_LIMIT)


def _mod_kernel(c_ref, w_ref, b_ref, o_ref, sc_scr):
    d, tn = w_ref.shape
    n_vec = c_ref.shape[1]
    sub, lanes = 8, sc_scr.shape[-1]

    @pl.when(pl.program_id(0) == 0)
    def _():
        sc = _silu(c_ref[...])
        for r in range(n_vec):
            sc_scr[r] = jnp.broadcast_to(sc[:, r:r + 1], (d, lanes))

    def body(i, accs):
        rows = pl.ds(pl.multiple_of(i * sub, sub), sub)
        w = w_ref[rows, :]
        return tuple(acc + w * jnp.concatenate([sc_scr[r, rows, :]] * (tn // lanes), axis=1)
                     for r, acc in enumerate(accs))

    zero = jnp.zeros((sub, tn), F32)
    accs = lax.fori_loop(0, d // sub, body, (zero,) * n_vec, unroll=8)
    for r, acc in enumerate(accs):
        o_ref[r:r + 1, :] = jnp.sum(acc, axis=0, keepdims=True) + b_ref[...]


def _mod_call(cc, w_mod, b_mod):
    d, n_vec = cc.shape
    n = w_mod.shape[1]
    tn = next(t for t in (1024, 512, 256, 128) if n % t == 0)
    return pl.pallas_call(
        _mod_kernel,
        grid=(n // tn,),
        in_specs=[pl.BlockSpec((d, n_vec), lambda j: (0, 0)),
                  pl.BlockSpec((d, tn), lambda j: (0, j)),
                  pl.BlockSpec((1, tn), lambda j: (0, j))],
        out_specs=pl.BlockSpec((n_vec, tn), lambda j: (0, j)),
        out_shape=jax.ShapeDtypeStruct((n_vec, n), F32),
        scratch_shapes=[pltpu.VMEM((n_vec, d, HEAD_DIM), F32)],
        compiler_params=_params("arbitrary"),
        name="mod",
    )(cc, w_mod, b_mod)


def _adaln(x, nw, mod_ref):
    return (_rmsnorm(x, nw) * (1.0 + mod_ref[1:2, :]) + mod_ref[0:1, :]).astype(BF16)


FFN_NEXT_STEP = 1


def _ffn_kernel(*refs, nf, final_norm, emit_weights, n_side):
    refs = list(refs)
    x_ref, mod_ref, nw_ref, wg_ref, wu_ref, wd_ref = refs[:6]
    del refs[:6]
    fnw_ref = refs.pop(0) if final_norm else None
    side_in = [refs.pop(0) for _ in range(n_side)]
    o_ref = refs.pop(0)
    w_out = [refs.pop(0) for _ in range(3)] if emit_weights else None
    side_out = [refs.pop(0) for _ in range(n_side)]
    u_scr, x_scr, acc_scr = refs
    i, f = pl.program_id(0), pl.program_id(1)
    slot = i % 2

    for src, dst in zip(side_in, side_out):
        dst[...] = src[...].astype(BF16)

    def take_block(dst):
        x = x_ref[...]
        x_scr[dst] = x
        u_scr[dst] = _adaln(x, nw_ref[...], mod_ref)

    @pl.when((i == 0) & (f == 0))
    def _():
        take_block(0)

    def tile_product():
        wg, wu, wd = wg_ref[...], wu_ref[...], wd_ref[...]
        if emit_weights:
            wg, wu, wd = wg.astype(BF16), wu.astype(BF16), wd.astype(BF16)
            for dst, w in zip(w_out, (wg, wu, wd)):
                dst[...] = w
        u = u_scr[slot]
        a = (_silu(_dot(u, wg)) * _dot(u, wu)).astype(BF16)
        return _dot(a, wd)

    with_next = (f == FFN_NEXT_STEP) & (i + 1 < pl.num_programs(0))

    @pl.when(f == 0)
    def _():
        acc_scr[...] = tile_product()

    @pl.when(with_next)
    def _():
        take_block(1 - slot)
        acc_scr[...] += tile_product()

    @pl.when((f > 0) & (f < nf - 1) & jnp.logical_not(with_next))
    def _():
        acc_scr[...] += tile_product()

    @pl.when(f == nf - 1)
    def _():
        h = x_scr[slot] + 0.5 * mod_ref[2:3, :] * (acc_scr[...] + tile_product())
        if final_norm:
            h = _rmsnorm(h, fnw_ref[...])
        o_ref[...] = h


def _side_cast_spec(shape, n_blk, nf):
    rows, cols = shape
    assert rows % (16 * n_blk) == 0
    n_col = next(c for c in range(nf, 0, -1) if cols % c == 0 and (cols // c) % HEAD_DIM == 0)
    return pl.BlockSpec((rows // n_blk, cols // n_col), lambda i, f: (i, jnp.minimum(f, n_col - 1)))


def _ffn_call(x, mod3, nw, wg, wu, wd, final_nw=None, side_casts=(), tm=512, tf=512):
    t, d = x.shape
    dff = wg.shape[1]
    tm = min(tm, t)
    nf, n_blk = dff // tf, t // tm
    assert t % tm == 0 and dff % tf == 0 and nf > FFN_NEXT_STEP + 1
    final_norm = final_nw is not None
    emit_weights = wg.dtype == F32
    assert not emit_weights or n_blk == 1
    x_row = lambda i, f: jnp.where((i == 0) & (f == 0), 0, jnp.minimum(i + 1, n_blk - 1))
    w_specs = [pl.BlockSpec((d, tf), lambda i, f: (0, f)),
               pl.BlockSpec((d, tf), lambda i, f: (0, f)),
               pl.BlockSpec((tf, d), lambda i, f: (f, 0))]
    side_specs = [_side_cast_spec(a.shape, n_blk, nf) for a in side_casts]
    in_specs = [pl.BlockSpec((tm, d), lambda i, f: (x_row(i, f), 0)),
                pl.BlockSpec((3, d), lambda i, f: (0, 0)),
                pl.BlockSpec((1, d), lambda i, f: (0, 0))] + w_specs
    args = [x, mod3, nw, wg, wu, wd]
    if final_norm:
        in_specs.append(pl.BlockSpec((1, d), lambda i, f: (0, 0)))
        args.append(final_nw)
    in_specs += side_specs
    args += list(side_casts)
    out_specs = [pl.BlockSpec((tm, d), lambda i, f: (i, 0))]
    out_shape = [jax.ShapeDtypeStruct((t, d), F32)]
    if emit_weights:
        out_specs += w_specs
        out_shape += [jax.ShapeDtypeStruct(w.shape, BF16) for w in (wg, wu, wd)]
    out_specs += side_specs
    out_shape += [jax.ShapeDtypeStruct(a.shape, BF16) for a in side_casts]
    return pl.pallas_call(
        functools.partial(_ffn_kernel, nf=nf, final_norm=final_norm, emit_weights=emit_weights,
                          n_side=len(side_casts)),
        grid=(n_blk, nf),
        in_specs=in_specs,
        out_specs=out_specs,
        out_shape=out_shape,
        scratch_shapes=[pltpu.VMEM((2, tm, d), BF16), pltpu.VMEM((2, tm, d), F32), pltpu.VMEM((tm, d), F32)],
        compiler_params=_params("arbitrary", "arbitrary"),
        name="ffn",
    )(*args)


def _inproj_kernel(x_ref, mod_ref, nw_ref, w_ref, o_ref, u_scr):
    i, j = pl.program_id(0), pl.program_id(1)
    slot = i % 2

    @pl.when((i == 0) & (j == 0))
    def _():
        u_scr[0] = _adaln(x_ref[...], nw_ref[...], mod_ref)

    with_next = (j == FFN_NEXT_STEP) & (i + 1 < pl.num_programs(0))

    @pl.when(with_next)
    def _():
        o_ref[...] = _dot(u_scr[slot], w_ref[...])
        u_scr[1 - slot] = _adaln(x_ref[...], nw_ref[...], mod_ref)

    @pl.when(jnp.logical_not(with_next))
    def _():
        o_ref[...] = _dot(u_scr[slot], w_ref[...])


def _inproj_call(x, mod2, nw, w, tm=1024, tn=1024):
    t, d = x.shape
    n = w.shape[1]
    tm = min(tm, t)
    n_blk = t // tm
    assert t % tm == 0 and n % tn == 0 and n // tn > FFN_NEXT_STEP
    x_row = lambda i, j: jnp.where((i == 0) & (j == 0), 0, jnp.minimum(i + 1, n_blk - 1))
    return pl.pallas_call(
        _inproj_kernel,
        grid=(n_blk, n // tn),
        in_specs=[pl.BlockSpec((tm, d), lambda i, j: (x_row(i, j), 0)),
                  pl.BlockSpec((2, d), lambda i, j: (0, 0)),
                  pl.BlockSpec((1, d), lambda i, j: (0, 0)),
                  pl.BlockSpec((d, tn), lambda i, j: (0, j))],
        out_specs=pl.BlockSpec((tm, tn), lambda i, j: (i, j)),
        out_shape=jax.ShapeDtypeStruct((t, n), F32),
        scratch_shapes=[pltpu.VMEM((2, tm, d), BF16)],
        compiler_params=_params("arbitrary", "arbitrary"),
        name="inproj",
    )(x, mod2, nw, w)


def _split3(x):
    hi = x.astype(BF16)
    r = x - hi.astype(F32)
    mid = r.astype(BF16)
    lo = (r - mid.astype(F32)).astype(BF16)
    return hi, mid, lo


def _hg_gates(q_raw, f_raw, lb, tri):
    q = _silu(q_raw)
    f = lb + (1.0 - lb) * _sigmoid(f_raw)
    hi, mid, lo = _split3(jnp.log(f))
    return q, 1.0 - f, f, _dot(tri, hi) + _dot(tri, mid) + _dot(tri, lo)


def _hg_level_operand(m, q, k, f, b, b_row, pos, rev):
    c = HG_CHUNK
    if m >= 8:
        qk, bmid = [], []
        for a in range(c // (2 * m)):
            lo_r, mid_r, hi_r = a * 2 * m, a * 2 * m + m, (a + 1) * 2 * m
            qk += [q[lo_r:mid_r], k[mid_r:hi_r]] if rev else [k[lo_r:mid_r], q[mid_r:hi_r]]
            bmid.append(jnp.broadcast_to(b_row(mid_r if rev else mid_r - 1), (2 * m, HEAD_DIM)))
        qk = jnp.concatenate(qk, axis=0)
        bmid = bmid[0] if len(bmid) == 1 else jnp.concatenate(bmid, axis=0)
        return (qk * jnp.exp(-jnp.abs(b - bmid))).astype(BF16)
    half = pos & (2 * m - 1)
    q_side = (half < m) if rev else (half >= m)
    if m == 4:
        rows = [jnp.broadcast_to(b_row(a * 8 + (4 if rev else 3)), (8, HEAD_DIM)) for a in range(c // 8)]
        w = jnp.exp(-jnp.abs(b - jnp.concatenate(rows, axis=0)))
    elif m == 2:
        f_next = pltpu.roll(f, c - 1, 0)
        f_prev = pltpu.roll(f, 1, 0)
        if rev:
            w = jnp.where(half == 0, f * f_next, jnp.where(half == 1, f, jnp.where(half == 2, 1.0, f_prev)))
        else:
            w = jnp.where(half == 0, f_next, jnp.where(half == 1, 1.0, jnp.where(half == 2, f, f * f_prev)))
    else:
        w = jnp.where(q_side, f, 1.0)
    return (jnp.where(q_side, q, k) * w).astype(BF16)


def _hg_state_step(st, k, vb, b, b_end):
    return st * jnp.exp(b_end) + _dot_tn(vb, (k * jnp.exp(b_end - b)).astype(BF16))


def _hg_ctx_kernel(q_ref, ff_ref, fb_ref, v_ref, lb_ref, tri_ref, o_ref, b_scr):
    c = HG_CHUNK
    n_chunks = q_ref.shape[0] // c
    for d, (f_ref, rev) in enumerate(((ff_ref, False), (fb_ref, True))):
        st = jnp.zeros((HEAD_DIM, HEAD_DIM), F32)
        for ci in (range(n_chunks - 1, -1, -1) if rev else range(n_chunks)):
            sl = pl.ds(ci * c, c)
            _, k, _, b = _hg_gates(q_ref[sl, :], f_ref[sl, :], lb_ref[d:d + 1, :], tri_ref[d])
            b_scr[...] = b
            st = _hg_state_step(st, k, v_ref[sl, :].astype(BF16), b, b_scr[pl.ds(0 if rev else c - 1, 1), :])
        o_ref[d] = st


def _hg_kernel(q_ref, f_ref, v_ref, st0_ref, lb_ref, tri_ref, lvl_ref, *rest, rev, heads_per_step, unroll,
               readout):
    if readout:
        of_ref, g_ref, nw_ref, o_ref, b_scr, st_scr = rest
    else:
        o_ref, b_scr, st_scr = rest
    c = HG_CHUNK

    @pl.when(pl.program_id(1) == 0)
    def _():
        st_scr[...] = st0_ref[...]

    n_chunks = q_ref.shape[0] // c
    tri = tri_ref[...]

    def body(i, carry):
        where = []
        for u in range(unroll):
            ci = i * unroll + u
            ci = (n_chunks - 1 - ci) if rev else ci
            rows = pl.ds(pl.multiple_of(ci * c, c), c)
            where += [(rows, slice(h * HEAD_DIM, (h + 1) * HEAD_DIM)) for h in range(heads_per_step)]
        n = len(where)
        gates = [_hg_gates(q_ref[r, cs], f_ref[r, cs], lb_ref[:, cs], tri) for r, cs in where]
        for p in range(n):
            b_scr[p] = gates[p][3]
        vbs = [v_ref[r, cs].astype(BF16) for r, cs in where]

        pos = lax.broadcasted_iota(jnp.int32, (c, HEAD_DIM), 0)
        lvl = lvl_ref[...]
        scores = [0.0] * n
        for m in HG_LEVELS:
            for p in range(n):
                q, k, f, b = gates[p]
                x = _hg_level_operand(m, q, k, f, b, lambda r, p=p: b_scr[p, pl.ds(r, 1), :], pos, rev)
                scores[p] = jnp.where(lvl == m, _dot_nt(x, x), scores[p])

        outs = [None] * n
        for h in range(heads_per_step):
            st = st_scr[h]
            for u in range(unroll):
                p = u * heads_per_step + h
                q, k, f, b = gates[p]
                outs[p] = _dot_nt((q * jnp.exp(b)).astype(BF16), st.astype(BF16))
                st = _hg_state_step(st, k, vbs[p], b, b_scr[p, pl.ds(0 if rev else c - 1, 1), :])
            st_scr[h] = st

        for p, (r, cs) in enumerate(where):
            q, k, f, b = gates[p]
            o = outs[p] + _dot(scores[p].astype(BF16), vbs[p])
            o = o + jnp.sum(q * k, axis=-1, keepdims=True) * v_ref[r, cs]
            if readout:
                o = _rmsnorm(of_ref[r, cs] + o, nw_ref[...]) * _silu(g_ref[r, cs])
            o_ref[r, cs] = o.astype(o_ref.dtype)
        return carry

    lax.fori_loop(0, n_chunks // unroll, body, 0)


def _hg_tables(rev):
    t = np.arange(HG_CHUNK)[:, None]
    s = np.arange(HG_CHUNK)[None, :]
    x = t ^ s
    msb = np.where(x > 0, 2 ** np.floor(np.log2(np.maximum(x, 1))).astype(np.int64), 0)
    lvl = np.where((s > t) if rev else (s < t), msb, 0).astype(np.int32)
    tri = np.tril(np.ones((HG_CHUNK, HG_CHUNK), np.float32))
    return (tri.T if rev else tri), lvl


def _hg_ctx_call(p_ctx, lb, *, n_heads):
    t_ctx = p_ctx.shape[0]
    assert t_ctx % HG_CHUNK == 0
    tri = jnp.asarray(np.stack([_hg_tables(False)[0], _hg_tables(True)[0]]), BF16)
    col = lambda c0: pl.BlockSpec((t_ctx, HEAD_DIM), lambda h: (0, c0 + h))
    const = pl.BlockSpec((2, HG_CHUNK, HG_CHUNK), lambda h: (0, 0, 0))
    return pl.pallas_call(
        _hg_ctx_kernel,
        grid=(n_heads,),
        in_specs=[col(0), col(n_heads), col(2 * n_heads), col(3 * n_heads),
                  pl.BlockSpec((2, HEAD_DIM), lambda h: (0, h)), const],
        out_specs=pl.BlockSpec((2, None, HEAD_DIM, HEAD_DIM), lambda h: (0, h, 0, 0)),
        out_shape=jax.ShapeDtypeStruct((2, n_heads, HEAD_DIM, HEAD_DIM), F32),
        scratch_shapes=[pltpu.VMEM((HG_CHUNK, HEAD_DIM), F32)],
        compiler_params=_params("parallel"),
        name="hgrn_ctx",
    )(p_ctx, p_ctx, p_ctx, p_ctx, lb, tri)


def _hg_call(p_lat, st0, lb_dir, o_fw, norm_w, *, rev, n_heads, tb=1024, heads_per_step=2, unroll=4):
    t_lat = p_lat.shape[0]
    tb = min(tb, t_lat)
    hb = heads_per_step
    assert t_lat % tb == 0 and tb % (HG_CHUNK * unroll) == 0 and n_heads % hb == 0
    n_blk, n_grp = t_lat // tb, n_heads // hb
    readout = o_fw is not None
    blk_row = (lambda j: n_blk - 1 - j) if rev else (lambda j: j)
    lat = lambda c0: pl.BlockSpec((tb, hb * HEAD_DIM), lambda g, j: (blk_row(j), c0 * n_grp + g))
    tri, lvl = _hg_tables(rev)
    const = pl.BlockSpec((HG_CHUNK, HG_CHUNK), lambda g, j: (0, 0))
    in_specs = [lat(0), lat(2 if rev else 1), lat(3),
                pl.BlockSpec((hb, HEAD_DIM, HEAD_DIM), lambda g, j: (g, 0, 0)),
                pl.BlockSpec((1, hb * HEAD_DIM), lambda g, j: (0, g)), const, const]
    args = [p_lat, p_lat, p_lat, st0, lb_dir, jnp.asarray(tri, BF16), jnp.asarray(lvl)]
    out_spec = pl.BlockSpec((tb, hb * HEAD_DIM), lambda g, j: (blk_row(j), g))
    if readout:
        in_specs += [out_spec, lat(4), pl.BlockSpec((1, HEAD_DIM), lambda g, j: (0, 0))]
        args += [o_fw, p_lat, norm_w]
    return pl.pallas_call(
        functools.partial(_hg_kernel, rev=rev, heads_per_step=hb, unroll=unroll, readout=readout),
        grid=(n_grp, n_blk),
        in_specs=in_specs,
        out_specs=out_spec,
        out_shape=jax.ShapeDtypeStruct((t_lat, n_heads * HEAD_DIM), BF16 if readout else F32),
        scratch_shapes=[pltpu.VMEM((unroll * hb, HG_CHUNK, HEAD_DIM), F32),
                        pltpu.VMEM((hb, HEAD_DIM, HEAD_DIM), F32)],
        compiler_params=_params("parallel", "arbitrary"),
        name="hgrn_bwd" if rev else "hgrn_fwd",
    )(*args)


def _rope(x, cos, sin_signed, low):
    return x * cos + jnp.where(low, pltpu.roll(x, 96, 1), pltpu.roll(x, 32, 1)) * sin_signed


def _na_kernel(q_ref, k_ref, v_ref, kc_ref, vc_ref, rtab_ref, ctab_ref, cbias_ref,
               o_ref, kt_scr, v_scr, kct_scr, bias_scr, *, rows):
    tile = 2 * GRID_W
    n_rb = rows // NA_QROWS
    lane = lax.broadcasted_iota(jnp.int32, (GRID_W, HEAD_DIM), 1)
    low = (lane & (HEAD_DIM // 2 - 1)) < HEAD_DIM // 4

    def rope_rows(x_ref, row0, n):
        out = []
        for i in range(n):
            x = x_ref[pl.ds(pl.multiple_of((row0 + i) * GRID_W, GRID_W), GRID_W), :]
            cos = rtab_ref[0, pl.ds(row0 + i, 1), :] + ctab_ref[0]
            sin = rtab_ref[1, pl.ds(row0 + i, 1), :] + ctab_ref[1]
            out.append(_rope(x, cos, sin, low))
        return jnp.concatenate(out, axis=0)

    def prep(i, carry):
        js = [i * NA_PREP_TILES + u for u in range(NA_PREP_TILES)]
        roped = [rope_rows(k_ref, 2 * j, 2) for j in js]
        for j, kr in zip(js, roped):
            kt_scr[j] = kr.T.astype(BF16)
            v_scr[j] = v_ref[pl.ds(pl.multiple_of(j * tile, tile), tile), :].astype(BF16)
        return carry

    lax.fori_loop(0, rows // (2 * NA_PREP_TILES), prep, 0)
    kct_scr[...] = kc_ref[...].T.astype(BF16)

    bias_scr[...] = jnp.full(bias_scr.shape, NEG_BIG, F32)
    for e in range(3):
        for a in range(NA_QROWS):
            b0 = (0, a, NA_KROWS - NA_KH)[e]
            d0 = (NA_KH - 1 - a, NA_KH // 2 - 1, NA_QROWS - 1 - a)[e]
            for i in range(NA_KH):
                bias_scr[e, a * GRID_W:(a + 1) * GRID_W, (b0 + i) * GRID_W:(b0 + i + 1) * GRID_W] = (
                    cbias_ref[d0 + i])

    vc = vc_ref[...].astype(BF16)

    def row_blocks(i, carry):
        rbs = [i * NA_STEP_BLOCKS + u for u in range(NA_STEP_BLOCKS)]
        qs = [(rope_rows(q_ref, rb * NA_QROWS, NA_QROWS) * (HEAD_DIM ** -0.5)).astype(BF16) for rb in rbs]
        j0s = [jnp.clip(rb * (NA_QROWS // 2) - NA_KH // 4, 0, (rows - NA_KROWS) // 2) for rb in rbs]
        s_w, s_c = [], []
        for rb, q, j0 in zip(rbs, qs, j0s):
            kw = jnp.concatenate([kt_scr[j0 + t] for t in range(NA_KROWS // 2)], axis=1)
            edge = jnp.where(rb == 0, 0, jnp.where(rb == n_rb - 1, 2, 1))
            s_w.append(_dot(q, kw) + bias_scr[edge])
            s_c.append(_dot(q, kct_scr[...]))
        nq = NA_QROWS * GRID_W
        for rb, j0, sw, sc in zip(rbs, j0s, s_w, s_c):
            m = jnp.maximum(jnp.max(sw, axis=-1, keepdims=True), jnp.max(sc, axis=-1, keepdims=True))
            p_w = jnp.exp(sw - m)
            p_c = jnp.exp(sc - m)
            denom = jnp.sum(p_w, axis=-1, keepdims=True) + jnp.sum(p_c, axis=-1, keepdims=True)
            vw = jnp.concatenate([v_scr[j0 + t] for t in range(NA_KROWS // 2)], axis=0)
            o = _dot(p_w.astype(BF16), vw) + _dot(p_c.astype(BF16), vc)
            o_ref[pl.ds(pl.multiple_of(rb * nq, nq), nq), :] = (o / denom).astype(o_ref.dtype)
        return carry

    lax.fori_loop(0, n_rb // NA_STEP_BLOCKS, row_blocks, 0)


def _na_col_bias(rpb):
    col = np.arange(GRID_W)
    col_start = np.clip(col - NA_KW // 2, 0, GRID_W - NA_KW)
    col_in = (col[None, :] >= col_start[:, None]) & (col[None, :] < col_start[:, None] + NA_KW)
    dc = np.clip(col[None, :] - col[:, None], 1 - NA_KW, NA_KW - 1) + (NA_KW - 1)
    onehot = (dc[None] == np.arange(2 * NA_KW - 1)[:, None, None]).astype(np.float32)
    vals = jnp.einsum("hrk,kqc->hrqc", rpb.astype(F32), onehot, precision=lax.Precision.HIGHEST)
    return jnp.where(col_in[None, None], vals, NEG_BIG)


def _rope_tables(rows):
    quarter = HEAD_DIM // 4
    inv = ROPE_BASE ** (-jnp.arange(quarter, dtype=F32) / quarter)
    ang_r = jnp.arange(rows).astype(F32)[:, None] * inv[None, :]
    ang_c = jnp.arange(GRID_W).astype(F32)[:, None] * inv[None, :]
    half = lambda a: jnp.stack([jnp.concatenate([jnp.cos(a), jnp.cos(a)], axis=-1),
                                jnp.concatenate([-jnp.sin(a), jnp.sin(a)], axis=-1)])
    r, c = half(ang_r), half(ang_c)
    return jnp.concatenate([r, jnp.zeros_like(r)], axis=-1), jnp.concatenate([jnp.zeros_like(c), c], axis=-1)


def _na_call(p_ctx, p_lat, rpb, *, col0, n_heads):
    t_ctx, t = p_ctx.shape[0], p_lat.shape[0]
    rows = t // GRID_W
    assert t % GRID_W == 0 and rows % (NA_QROWS * NA_STEP_BLOCKS) == 0 and rows >= NA_KROWS + NA_QROWS
    assert rows % (2 * NA_PREP_TILES) == 0
    assert NA_QROWS % 2 == 0 and NA_KH % 4 == 0
    nq = NA_QROWS * GRID_W
    rtab, ctab = _rope_tables(rows)
    cbias = _na_col_bias(rpb)
    full = lambda col: pl.BlockSpec((t, HEAD_DIM), lambda h: (0, col + h))
    ctx = lambda col: pl.BlockSpec((t_ctx, HEAD_DIM), lambda h: (0, col + h))
    return pl.pallas_call(
        functools.partial(_na_kernel, rows=rows),
        grid=(n_heads,),
        in_specs=[full(col0), full(col0 + n_heads), full(col0 + 2 * n_heads),
                  ctx(col0 + n_heads), ctx(col0 + 2 * n_heads),
                  pl.BlockSpec((2, rows, HEAD_DIM), lambda h: (0, 0, 0)),
                  pl.BlockSpec((2, GRID_W, HEAD_DIM), lambda h: (0, 0, 0)),
                  pl.BlockSpec((None, 2 * NA_KH - 1, GRID_W, GRID_W), lambda h: (h, 0, 0, 0))],
        out_specs=pl.BlockSpec((t, HEAD_DIM), lambda h: (0, h)),
        out_shape=jax.ShapeDtypeStruct((t, n_heads * HEAD_DIM), BF16),
        scratch_shapes=[pltpu.VMEM((rows // 2, HEAD_DIM, 2 * GRID_W), BF16),
                        pltpu.VMEM((rows // 2, 2 * GRID_W, HEAD_DIM), BF16),
                        pltpu.VMEM((HEAD_DIM, t_ctx), BF16),
                        pltpu.VMEM((3, nq, NA_KROWS * GRID_W), F32)],
        compiler_params=_params("parallel"),
        name="natten",
    )(p_lat, p_lat, p_lat, p_ctx, p_ctx, rtab, ctab, cbias)


def _outproj_kernel(hg_ref, na_ref, h_ref, gate_ref, w_ref, o_ref):
    hw = hg_ref.shape[1]
    mix = _dot(hg_ref[...], w_ref[:hw, :]) + _dot(na_ref[...], w_ref[hw:, :])
    o_ref[...] = h_ref[...] + gate_ref[...] * mix


def _outproj_call(hg, na, h, gate, w, tm=512):
    t, d = h.shape
    hw, nw = hg.shape[1], na.shape[1]
    assert t % tm == 0
    return pl.pallas_call(
        _outproj_kernel,
        grid=(t // tm,),
        in_specs=[pl.BlockSpec((tm, hw), lambda i: (i, 0)),
                  pl.BlockSpec((tm, nw), lambda i: (i, 0)),
                  pl.BlockSpec((tm, d), lambda i: (i, 0)),
                  pl.BlockSpec((1, d), lambda i: (0, 0)),
                  pl.BlockSpec((hw + nw, d), lambda i: (0, 0))],
        out_specs=pl.BlockSpec((tm, d), lambda i: (i, 0)),
        out_shape=jax.ShapeDtypeStruct((t, d), F32),
        compiler_params=_params("parallel"),
        name="outproj",
    )(hg, na, h, gate, w)


def kernel(x, c, ctx, c_ctx, w_mod, b_mod, norm_w, ffn1_w_gate, ffn1_w_up, ffn1_w_down, w_in, hg_lb_logits,
           hg_norm_w, na_rpb, w_out, ffn2_w_gate, ffn2_w_up, ffn2_w_down, final_norm_w):
    batch, _, d = x.shape
    depth = w_mod.shape[0]
    assert batch == 1 and depth == 1
    hg_heads = hg_lb_logits.shape[-1] // HEAD_DIM
    na_heads = na_rpb.shape[1]
    row = lambda v: v.reshape(1, -1)

    cc = jnp.stack([c[0], c_ctx], axis=1)
    mods = _mod_call(cc, w_mod[0], row(b_mod[0]))
    m_l = mods[0].reshape(N_MOD, d)
    m_c = mods[1].reshape(N_MOD, d)
    lb = jnp.cumsum(jax.nn.softmax(hg_lb_logits.astype(F32), axis=1), axis=1)[:, 0]

    h_c, *ffn1 = _ffn_call(ctx[0], m_c[0:3], row(norm_w[0, 0]), ffn1_w_gate[0], ffn1_w_up[0], ffn1_w_down[0])
    h_l, *later = _ffn_call(x[0], m_l[0:3], row(norm_w[0, 0]), *ffn1,
                            side_casts=(ffn2_w_gate[0], ffn2_w_up[0], ffn2_w_down[0], w_in[0], w_out[0]))
    ffn2, (w_in_b, w_out_b) = later[:3], later[3:]
    p_l = _inproj_call(h_l, m_l[3:5], row(norm_w[0, 1]), w_in_b)
    p_c = _inproj_call(h_c, m_c[3:5], row(norm_w[0, 1]), w_in_b)

    st0 = _hg_ctx_call(p_c, lb, n_heads=hg_heads)
    o_fw = _hg_call(p_l, st0[0], lb[0:1], None, None, rev=False, n_heads=hg_heads)
    hg = _hg_call(p_l, st0[1], lb[1:2], o_fw, row(hg_norm_w[0]), rev=True, n_heads=hg_heads)
    na = _na_call(p_c, p_l, na_rpb[0], col0=5 * hg_heads, n_heads=na_heads)

    h_l = _outproj_call(hg, na, h_l, m_l[5:6], w_out_b)
    out, = _ffn_call(h_l, m_l[6:9], row(norm_w[0, 2]), *ffn2, final_nw=row(final_norm_w))
    return out[None]
```

```python
import functools

import numpy as np
import jax
import jax.numpy as jnp
from jax import lax
from jax.experimental import pallas as pl
from jax.experimental.pallas import tpu as pltpu

F32 = jnp.float32
BF16 = jnp.bfloat16

EPS = 1e-6
GRID_W = 64
HEAD_DIM = 128
NA_KH = 8
NA_KW = 16
ROPE_BASE = 10000.0
N_MOD = 9
NEG_BIG = -1e30

V7X_VMEM_BYTES = 64 * 1024 * 1024
VMEM_LIMIT = V7X_VMEM_BYTES - 8 * 1024 * 1024

HG_CHUNK = 128
HG_LEVELS = (64, 32, 16, 8, 4, 2, 1)
NA_QROWS = 4
NA_KROWS = NA_QROWS + NA_KH
NA_STEP_BLOCKS = 4
NA_PREP_TILES = 4


def _dot(a, b):
    return jnp.dot(a, b, preferred_element_type=F32)


def _dot_nt(a, b):
    return lax.dot_general(a, b, (((1,), (1,)), ((), ())), preferred_element_type=F32)


def _dot_tn(a, b):
    return lax.dot_general(a, b, (((0,), (0,)), ((), ())), preferred_element_type=F32)


def _sigmoid(x):
    return 0.5 * jnp.tanh(0.5 * x) + 0.5


def _silu(x):
    return x * _sigmoid(x)


def _rmsnorm(x, w):
    return x * lax.rsqrt(jnp.mean(x * x, axis=-1, keepdims=True) + EPS) * w


def _params(*semantics):
    return pltpu.CompilerParams(dimension_semantics=semantics, vmem_limit_bytes=VMEM_LIMIT)


def _mod_kernel(c_ref, w_ref, b_ref, o_ref, sc_scr):
    d, tn = w_ref.shape
    n_vec = c_ref.shape[1]
    sub, lanes = 8, sc_scr.shape[-1]

    @pl.when(pl.program_id(0) == 0)
    def _():
        sc = _silu(c_ref[...])
        for r in range(n_vec):
            sc_scr[r] = jnp.broadcast_to(sc[:, r:r + 1], (d, lanes))

    def body(i, accs):
        rows = pl.ds(pl.multiple_of(i * sub, sub), sub)
        w = w_ref[rows, :]
        return tuple(acc + w * jnp.concatenate([sc_scr[r, rows, :]] * (tn // lanes), axis=1)
                     for r, acc in enumerate(accs))

    zero = jnp.zeros((sub, tn), F32)
    accs = lax.fori_loop(0, d // sub, body, (zero,) * n_vec, unroll=8)
    for r, acc in enumerate(accs):
        o_ref[r:r + 1, :] = jnp.sum(acc, axis=0, keepdims=True) + b_ref[...]


def _mod_call(cc, w_mod, b_mod):
    d, n_vec = cc.shape
    n = w_mod.shape[1]
    tn = next(t for t in (1024, 512, 256, 128) if n % t == 0)
    return pl.pallas_call(
        _mod_kernel,
        grid=(n // tn,),
        in_specs=[pl.BlockSpec((d, n_vec), lambda j: (0, 0)),
                  pl.BlockSpec((d, tn), lambda j: (0, j)),
                  pl.BlockSpec((1, tn), lambda j: (0, j))],
        out_specs=pl.BlockSpec((n_vec, tn), lambda j: (0, j)),
        out_shape=jax.ShapeDtypeStruct((n_vec, n), F32),
        scratch_shapes=[pltpu.VMEM((n_vec, d, HEAD_DIM), F32)],
        compiler_params=_params("arbitrary"),
        name="mod",
    )(cc, w_mod, b_mod)


def _adaln(x, nw, mod_ref):
    return (_rmsnorm(x, nw) * (1.0 + mod_ref[1:2, :]) + mod_ref[0:1, :]).astype(BF16)


def _ffn_kernel(*refs, nf, final_norm, emit_weights, n_side):
    refs = list(refs)
    x_ref, mod_ref, nw_ref, wg_ref, wu_ref, wd_ref = refs[:6]
    del refs[:6]
    fnw_ref = refs.pop(0) if final_norm else None
    side_in = [refs.pop(0) for _ in range(n_side)]
    o_ref = refs.pop(0)
    w_out = [refs.pop(0) for _ in range(3)] if emit_weights else None
    side_out = [refs.pop(0) for _ in range(n_side)]
    u_scr, acc_scr = refs
    f = pl.program_id(1)

    for src, dst in zip(side_in, side_out):
        dst[...] = src[...].astype(BF16)

    def tile_product(u):
        wg, wu, wd = wg_ref[...], wu_ref[...], wd_ref[...]
        if emit_weights:
            wg, wu, wd = wg.astype(BF16), wu.astype(BF16), wd.astype(BF16)
            for dst, w in zip(w_out, (wg, wu, wd)):
                dst[...] = w
        a = (_silu(_dot(u, wg)) * _dot(u, wu)).astype(BF16)
        return _dot(a, wd)

    @pl.when(f == 0)
    def _():
        u = _adaln(x_ref[...], nw_ref[...], mod_ref)
        u_scr[...] = u
        acc_scr[...] = tile_product(u)

    @pl.when((f > 0) & (f < nf - 1))
    def _():
        acc_scr[...] += tile_product(u_scr[...])

    @pl.when(f == nf - 1)
    def _():
        h = x_ref[...] + 0.5 * mod_ref[2:3, :] * (acc_scr[...] + tile_product(u_scr[...]))
        if final_norm:
            h = _rmsnorm(h, fnw_ref[...])
        o_ref[...] = h


def _side_cast_spec(shape, n_blk, nf):
    rows, cols = shape
    assert rows % (16 * n_blk) == 0
    n_col = next(c for c in range(nf, 0, -1) if cols % c == 0 and (cols // c) % HEAD_DIM == 0)
    return pl.BlockSpec((rows // n_blk, cols // n_col), lambda i, f: (i, jnp.minimum(f, n_col - 1)))


def _ffn_call(x, mod3, nw, wg, wu, wd, final_nw=None, side_casts=(), tm=512, tf=512):
    t, d = x.shape
    dff = wg.shape[1]
    tm = min(tm, t)
    nf, n_blk = dff // tf, t // tm
    assert t % tm == 0 and dff % tf == 0 and nf >= 2
    final_norm = final_nw is not None
    emit_weights = wg.dtype == F32
    assert not emit_weights or n_blk == 1
    w_specs = [pl.BlockSpec((d, tf), lambda i, f: (0, f)),
               pl.BlockSpec((d, tf), lambda i, f: (0, f)),
               pl.BlockSpec((tf, d), lambda i, f: (f, 0))]
    side_specs = [_side_cast_spec(a.shape, n_blk, nf) for a in side_casts]
    in_specs = [pl.BlockSpec((tm, d), lambda i, f: (i, 0)),
                pl.BlockSpec((3, d), lambda i, f: (0, 0)),
                pl.BlockSpec((1, d), lambda i, f: (0, 0))] + w_specs
    args = [x, mod3, nw, wg, wu, wd]
    if final_norm:
        in_specs.append(pl.BlockSpec((1, d), lambda i, f: (0, 0)))
        args.append(final_nw)
    in_specs += side_specs
    args += list(side_casts)
    out_specs = [pl.BlockSpec((tm, d), lambda i, f: (i, 0))]
    out_shape = [jax.ShapeDtypeStruct((t, d), F32)]
    if emit_weights:
        out_specs += w_specs
        out_shape += [jax.ShapeDtypeStruct(w.shape, BF16) for w in (wg, wu, wd)]
    out_specs += side_specs
    out_shape += [jax.ShapeDtypeStruct(a.shape, BF16) for a in side_casts]
    return pl.pallas_call(
        functools.partial(_ffn_kernel, nf=nf, final_norm=final_norm, emit_weights=emit_weights,
                          n_side=len(side_casts)),
        grid=(n_blk, nf),
        in_specs=in_specs,
        out_specs=out_specs,
        out_shape=out_shape,
        scratch_shapes=[pltpu.VMEM((tm, d), BF16), pltpu.VMEM((tm, d), F32)],
        compiler_params=_params("parallel", "arbitrary"),
        name="ffn",
    )(*args)


def _inproj_kernel(x_ref, mod_ref, nw_ref, w_ref, o_ref, u_scr):
    @pl.when(pl.program_id(1) == 0)
    def _():
        u = _adaln(x_ref[...], nw_ref[...], mod_ref)
        u_scr[...] = u
        o_ref[...] = _dot(u, w_ref[...])

    @pl.when(pl.program_id(1) > 0)
    def _():
        o_ref[...] = _dot(u_scr[...], w_ref[...])


def _inproj_call(x, mod2, nw, w, tm=1024, tn=1024):
    t, d = x.shape
    n = w.shape[1]
    tm = min(tm, t)
    assert t % tm == 0 and n % tn == 0
    return pl.pallas_call(
        _inproj_kernel,
        grid=(t // tm, n // tn),
        in_specs=[pl.BlockSpec((tm, d), lambda i, j: (i, 0)),
                  pl.BlockSpec((2, d), lambda i, j: (0, 0)),
                  pl.BlockSpec((1, d), lambda i, j: (0, 0)),
                  pl.BlockSpec((d, tn), lambda i, j: (0, j))],
        out_specs=pl.BlockSpec((tm, tn), lambda i, j: (i, j)),
        out_shape=jax.ShapeDtypeStruct((t, n), F32),
        scratch_shapes=[pltpu.VMEM((tm, d), BF16)],
        compiler_params=_params("parallel", "arbitrary"),
        name="inproj",
    )(x, mod2, nw, w)


def _split3(x):
    hi = x.astype(BF16)
    r = x - hi.astype(F32)
    mid = r.astype(BF16)
    lo = (r - mid.astype(F32)).astype(BF16)
    return hi, mid, lo


def _hg_gates(q_raw, f_raw, lb, tri):
    q = _silu(q_raw)
    f = lb + (1.0 - lb) * _sigmoid(f_raw)
    hi, mid, lo = _split3(jnp.log(f))
    return q, 1.0 - f, f, _dot(tri, hi) + _dot(tri, mid) + _dot(tri, lo)


def _hg_level_operand(m, q, k, f, b, b_row, pos, rev):
    c = HG_CHUNK
    if m >= 8:
        qk, bmid = [], []
        for a in range(c // (2 * m)):
            lo_r, mid_r, hi_r = a * 2 * m, a * 2 * m + m, (a + 1) * 2 * m
            qk += [q[lo_r:mid_r], k[mid_r:hi_r]] if rev else [k[lo_r:mid_r], q[mid_r:hi_r]]
            bmid.append(jnp.broadcast_to(b_row(mid_r if rev else mid_r - 1), (2 * m, HEAD_DIM)))
        qk = jnp.concatenate(qk, axis=0)
        bmid = bmid[0] if len(bmid) == 1 else jnp.concatenate(bmid, axis=0)
        return (qk * jnp.exp(-jnp.abs(b - bmid))).astype(BF16)
    half = pos & (2 * m - 1)
    q_side = (half < m) if rev else (half >= m)
    if m == 4:
        rows = [jnp.broadcast_to(b_row(a * 8 + (4 if rev else 3)), (8, HEAD_DIM)) for a in range(c // 8)]
        w = jnp.exp(-jnp.abs(b - jnp.concatenate(rows, axis=0)))
    elif m == 2:
        f_next = pltpu.roll(f, c - 1, 0)
        f_prev = pltpu.roll(f, 1, 0)
        if rev:
            w = jnp.where(half == 0, f * f_next, jnp.where(half == 1, f, jnp.where(half == 2, 1.0, f_prev)))
        else:
            w = jnp.where(half == 0, f_next, jnp.where(half == 1, 1.0, jnp.where(half == 2, f, f * f_prev)))
    else:
        w = jnp.where(q_side, f, 1.0)
    return (jnp.where(q_side, q, k) * w).astype(BF16)


def _hg_state_step(st, k, vb, b, b_end):
    return st * jnp.exp(b_end) + _dot_tn(vb, (k * jnp.exp(b_end - b)).astype(BF16))


def _hg_ctx_kernel(q_ref, ff_ref, fb_ref, v_ref, lb_ref, tri_ref, o_ref, b_scr):
    c = HG_CHUNK
    n_chunks = q_ref.shape[0] // c
    for d, (f_ref, rev) in enumerate(((ff_ref, False), (fb_ref, True))):
        st = jnp.zeros((HEAD_DIM, HEAD_DIM), F32)
        for ci in (range(n_chunks - 1, -1, -1) if rev else range(n_chunks)):
            sl = pl.ds(ci * c, c)
            _, k, _, b = _hg_gates(q_ref[sl, :], f_ref[sl, :], lb_ref[d:d + 1, :], tri_ref[d])
            b_scr[...] = b
            st = _hg_state_step(st, k, v_ref[sl, :].astype(BF16), b, b_scr[pl.ds(0 if rev else c - 1, 1), :])
        o_ref[d] = st


def _hg_kernel(q_ref, f_ref, v_ref, st0_ref, lb_ref, tri_ref, lvl_ref, *rest, rev, heads_per_step, unroll,
               readout, n_side):
    rest = list(rest)
    of_ref, g_ref, nw_ref = (rest.pop(0), rest.pop(0), rest.pop(0)) if readout else (None, None, None)
    side_in = [rest.pop(0) for _ in range(n_side)]
    o_ref = rest.pop(0)
    side_out = [rest.pop(0) for _ in range(n_side)]
    b_scr, st_scr = rest
    c = HG_CHUNK

    for src, dst in zip(side_in, side_out):
        dst[...] = src[...].astype(BF16)

    @pl.when(pl.program_id(1) == 0)
    def _():
        st_scr[...] = st0_ref[...]

    n_chunks = q_ref.shape[0] // c
    tri = tri_ref[...]

    def body(i, carry):
        where = []
        for u in range(unroll):
            ci = i * unroll + u
            ci = (n_chunks - 1 - ci) if rev else ci
            rows = pl.ds(pl.multiple_of(ci * c, c), c)
            where += [(rows, slice(h * HEAD_DIM, (h + 1) * HEAD_DIM)) for h in range(heads_per_step)]
        n = len(where)
        gates = [_hg_gates(q_ref[r, cs], f_ref[r, cs], lb_ref[:, cs], tri) for r, cs in where]
        for p in range(n):
            b_scr[p] = gates[p][3]
        vbs = [v_ref[r, cs].astype(BF16) for r, cs in where]

        pos = lax.broadcasted_iota(jnp.int32, (c, HEAD_DIM), 0)
        lvl = lvl_ref[...]
        scores = [0.0] * n
        for m in HG_LEVELS:
            for p in range(n):
                q, k, f, b = gates[p]
                x = _hg_level_operand(m, q, k, f, b, lambda r, p=p: b_scr[p, pl.ds(r, 1), :], pos, rev)
                scores[p] = jnp.where(lvl == m, _dot_nt(x, x), scores[p])

        outs = [None] * n
        for h in range(heads_per_step):
            st = st_scr[h]
            for u in range(unroll):
                p = u * heads_per_step + h
                q, k, f, b = gates[p]
                outs[p] = _dot_nt((q * jnp.exp(b)).astype(BF16), st.astype(BF16))
                st = _hg_state_step(st, k, vbs[p], b, b_scr[p, pl.ds(0 if rev else c - 1, 1), :])
            st_scr[h] = st

        for p, (r, cs) in enumerate(where):
            q, k, f, b = gates[p]
            o = outs[p] + _dot(scores[p].astype(BF16), vbs[p])
            o = o + jnp.sum(q * k, axis=-1, keepdims=True) * v_ref[r, cs]
            if readout:
                o = _rmsnorm(of_ref[r, cs] + o, nw_ref[...]) * _silu(g_ref[r, cs])
            o_ref[r, cs] = o.astype(o_ref.dtype)
        return carry

    lax.fori_loop(0, n_chunks // unroll, body, 0)


def _hg_tables(rev):
    t = np.arange(HG_CHUNK)[:, None]
    s = np.arange(HG_CHUNK)[None, :]
    x = t ^ s
    msb = np.where(x > 0, 2 ** np.floor(np.log2(np.maximum(x, 1))).astype(np.int64), 0)
    lvl = np.where((s > t) if rev else (s < t), msb, 0).astype(np.int32)
    tri = np.tril(np.ones((HG_CHUNK, HG_CHUNK), np.float32))
    return (tri.T if rev else tri), lvl


def _hg_ctx_call(p_ctx, lb, *, n_heads):
    t_ctx = p_ctx.shape[0]
    assert t_ctx % HG_CHUNK == 0
    tri = jnp.asarray(np.stack([_hg_tables(False)[0], _hg_tables(True)[0]]), BF16)
    col = lambda c0: pl.BlockSpec((t_ctx, HEAD_DIM), lambda h: (0, c0 + h))
    const = pl.BlockSpec((2, HG_CHUNK, HG_CHUNK), lambda h: (0, 0, 0))
    return pl.pallas_call(
        _hg_ctx_kernel,
        grid=(n_heads,),
        in_specs=[col(0), col(n_heads), col(2 * n_heads), col(3 * n_heads),
                  pl.BlockSpec((2, HEAD_DIM), lambda h: (0, h)), const],
        out_specs=pl.BlockSpec((2, None, HEAD_DIM, HEAD_DIM), lambda h: (0, h, 0, 0)),
        out_shape=jax.ShapeDtypeStruct((2, n_heads, HEAD_DIM, HEAD_DIM), F32),
        scratch_shapes=[pltpu.VMEM((HG_CHUNK, HEAD_DIM), F32)],
        compiler_params=_params("parallel"),
        name="hgrn_ctx",
    )(p_ctx, p_ctx, p_ctx, p_ctx, lb, tri)


def _hg_call(p_lat, st0, lb_dir, o_fw, norm_w, *, rev, n_heads, side_casts=(), tb=1024, heads_per_step=2,
             unroll=4):
    t_lat = p_lat.shape[0]
    tb = min(tb, t_lat)
    hb = heads_per_step
    assert t_lat % tb == 0 and tb % (HG_CHUNK * unroll) == 0 and n_heads % hb == 0
    n_blk, n_grp = t_lat // tb, n_heads // hb
    readout = o_fw is not None
    blk_row = (lambda j: n_blk - 1 - j) if rev else (lambda j: j)
    lat = lambda c0: pl.BlockSpec((tb, hb * HEAD_DIM), lambda g, j: (blk_row(j), c0 * n_grp + g))
    tri, lvl = _hg_tables(rev)
    const = pl.BlockSpec((HG_CHUNK, HG_CHUNK), lambda g, j: (0, 0))
    in_specs = [lat(0), lat(2 if rev else 1), lat(3),
                pl.BlockSpec((hb, HEAD_DIM, HEAD_DIM), lambda g, j: (g, 0, 0)),
                pl.BlockSpec((1, hb * HEAD_DIM), lambda g, j: (0, g)), const, const]
    args = [p_lat, p_lat, p_lat, st0, lb_dir, jnp.asarray(tri, BF16), jnp.asarray(lvl)]
    out_spec = pl.BlockSpec((tb, hb * HEAD_DIM), lambda g, j: (blk_row(j), g))
    if readout:
        in_specs += [out_spec, lat(4), pl.BlockSpec((1, HEAD_DIM), lambda g, j: (0, 0))]
        args += [o_fw, p_lat, norm_w]
    side_specs = [_side_cast_spec(a.shape, n_grp, n_blk) for a in side_casts]
    return pl.pallas_call(
        functools.partial(_hg_kernel, rev=rev, heads_per_step=hb, unroll=unroll, readout=readout,
                          n_side=len(side_casts)),
        grid=(n_grp, n_blk),
        in_specs=in_specs + side_specs,
        out_specs=[out_spec] + side_specs,
        out_shape=[jax.ShapeDtypeStruct((t_lat, n_heads * HEAD_DIM), BF16 if readout else F32)]
        + [jax.ShapeDtypeStruct(a.shape, BF16) for a in side_casts],
        scratch_shapes=[pltpu.VMEM((unroll * hb, HG_CHUNK, HEAD_DIM), F32),
                        pltpu.VMEM((hb, HEAD_DIM, HEAD_DIM), F32)],
        compiler_params=_params("parallel", "arbitrary"),
        name="hgrn_bwd" if rev else "hgrn_fwd",
    )(*args, *side_casts)


def _rope(x, cos, sin_signed, low):
    return x * cos + jnp.where(low, pltpu.roll(x, 96, 1), pltpu.roll(x, 32, 1)) * sin_signed


def _na_kernel(q_ref, k_ref, v_ref, kc_ref, vc_ref, rtab_ref, ctab_ref, cbias_ref,
               o_ref, kt_scr, v_scr, kct_scr, bias_scr, *, rows):
    tile = 2 * GRID_W
    n_rb = rows // NA_QROWS
    lane = lax.broadcasted_iota(jnp.int32, (GRID_W, HEAD_DIM), 1)
    low = (lane & (HEAD_DIM // 2 - 1)) < HEAD_DIM // 4

    def rope_rows(x_ref, row0, n):
        out = []
        for i in range(n):
            x = x_ref[pl.ds(pl.multiple_of((row0 + i) * GRID_W, GRID_W), GRID_W), :]
            cos = rtab_ref[0, pl.ds(row0 + i, 1), :] + ctab_ref[0]
            sin = rtab_ref[1, pl.ds(row0 + i, 1), :] + ctab_ref[1]
            out.append(_rope(x, cos, sin, low))
        return jnp.concatenate(out, axis=0)

    def prep(i, carry):
        js = [i * NA_PREP_TILES + u for u in range(NA_PREP_TILES)]
        roped = [rope_rows(k_ref, 2 * j, 2) for j in js]
        for j, kr in zip(js, roped):
            kt_scr[j] = kr.T.astype(BF16)
            v_scr[j] = v_ref[pl.ds(pl.multiple_of(j * tile, tile), tile), :].astype(BF16)
        return carry

    lax.fori_loop(0, rows // (2 * NA_PREP_TILES), prep, 0)
    kct_scr[...] = kc_ref[...].T.astype(BF16)

    bias_scr[...] = jnp.full(bias_scr.shape, NEG_BIG, F32)
    for e in range(3):
        for a in range(NA_QROWS):
            b0 = (0, a, NA_KROWS - NA_KH)[e]
            d0 = (NA_KH - 1 - a, NA_KH // 2 - 1, NA_QROWS - 1 - a)[e]
            for i in range(NA_KH):
                bias_scr[e, a * GRID_W:(a + 1) * GRID_W, (b0 + i) * GRID_W:(b0 + i + 1) * GRID_W] = (
                    cbias_ref[d0 + i])

    vc = vc_ref[...].astype(BF16)

    def row_blocks(i, carry):
        rbs = [i * NA_STEP_BLOCKS + u for u in range(NA_STEP_BLOCKS)]
        qs = [(rope_rows(q_ref, rb * NA_QROWS, NA_QROWS) * (HEAD_DIM ** -0.5)).astype(BF16) for rb in rbs]
        j0s = [jnp.clip(rb * (NA_QROWS // 2) - NA_KH // 4, 0, (rows - NA_KROWS) // 2) for rb in rbs]
        s_w, s_c = [], []
        for rb, q, j0 in zip(rbs, qs, j0s):
            kw = jnp.concatenate([kt_scr[j0 + t] for t in range(NA_KROWS // 2)], axis=1)
            edge = jnp.where(rb == 0, 0, jnp.where(rb == n_rb - 1, 2, 1))
            s_w.append(_dot(q, kw) + bias_scr[edge])
            s_c.append(_dot(q, kct_scr[...]))
        nq = NA_QROWS * GRID_W
        for rb, j0, sw, sc in zip(rbs, j0s, s_w, s_c):
            m = jnp.maximum(jnp.max(sw, axis=-1, keepdims=True), jnp.max(sc, axis=-1, keepdims=True))
            p_w = jnp.exp(sw - m)
            p_c = jnp.exp(sc - m)
            denom = jnp.sum(p_w, axis=-1, keepdims=True) + jnp.sum(p_c, axis=-1, keepdims=True)
            vw = jnp.concatenate([v_scr[j0 + t] for t in range(NA_KROWS // 2)], axis=0)
            o = _dot(p_w.astype(BF16), vw) + _dot(p_c.astype(BF16), vc)
            o_ref[pl.ds(pl.multiple_of(rb * nq, nq), nq), :] = (o / denom).astype(o_ref.dtype)
        return carry

    lax.fori_loop(0, n_rb // NA_STEP_BLOCKS, row_blocks, 0)


def _na_col_bias(rpb):
    col = np.arange(GRID_W)
    col_start = np.clip(col - NA_KW // 2, 0, GRID_W - NA_KW)
    col_in = (col[None, :] >= col_start[:, None]) & (col[None, :] < col_start[:, None] + NA_KW)
    dc = np.clip(col[None, :] - col[:, None], 1 - NA_KW, NA_KW - 1) + (NA_KW - 1)
    onehot = (dc[None] == np.arange(2 * NA_KW - 1)[:, None, None]).astype(np.float32)
    vals = jnp.einsum("hrk,kqc->hrqc", rpb.astype(F32), onehot, precision=lax.Precision.HIGHEST)
    return jnp.where(col_in[None, None], vals, NEG_BIG)


def _rope_tables(rows):
    quarter = HEAD_DIM // 4
    inv = ROPE_BASE ** (-jnp.arange(quarter, dtype=F32) / quarter)
    ang_r = jnp.arange(rows).astype(F32)[:, None] * inv[None, :]
    ang_c = jnp.arange(GRID_W).astype(F32)[:, None] * inv[None, :]
    half = lambda a: jnp.stack([jnp.concatenate([jnp.cos(a), jnp.cos(a)], axis=-1),
                                jnp.concatenate([-jnp.sin(a), jnp.sin(a)], axis=-1)])
    r, c = half(ang_r), half(ang_c)
    return jnp.concatenate([r, jnp.zeros_like(r)], axis=-1), jnp.concatenate([jnp.zeros_like(c), c], axis=-1)


def _na_call(p_ctx, p_lat, rpb, *, col0, n_heads):
    t_ctx, t = p_ctx.shape[0], p_lat.shape[0]
    rows = t // GRID_W
    assert t % GRID_W == 0 and rows % (NA_QROWS * NA_STEP_BLOCKS) == 0 and rows >= NA_KROWS + NA_QROWS
    assert rows % (2 * NA_PREP_TILES) == 0
    assert NA_QROWS % 2 == 0 and NA_KH % 4 == 0
    nq = NA_QROWS * GRID_W
    rtab, ctab = _rope_tables(rows)
    cbias = _na_col_bias(rpb)
    full = lambda col: pl.BlockSpec((t, HEAD_DIM), lambda h: (0, col + h))
    ctx = lambda col: pl.BlockSpec((t_ctx, HEAD_DIM), lambda h: (0, col + h))
    return pl.pallas_call(
        functools.partial(_na_kernel, rows=rows),
        grid=(n_heads,),
        in_specs=[full(col0), full(col0 + n_heads), full(col0 + 2 * n_heads),
                  ctx(col0 + n_heads), ctx(col0 + 2 * n_heads),
                  pl.BlockSpec((2, rows, HEAD_DIM), lambda h: (0, 0, 0)),
                  pl.BlockSpec((2, GRID_W, HEAD_DIM), lambda h: (0, 0, 0)),
                  pl.BlockSpec((None, 2 * NA_KH - 1, GRID_W, GRID_W), lambda h: (h, 0, 0, 0))],
        out_specs=pl.BlockSpec((t, HEAD_DIM), lambda h: (0, h)),
        out_shape=jax.ShapeDtypeStruct((t, n_heads * HEAD_DIM), BF16),
        scratch_shapes=[pltpu.VMEM((rows // 2, HEAD_DIM, 2 * GRID_W), BF16),
                        pltpu.VMEM((rows // 2, 2 * GRID_W, HEAD_DIM), BF16),
                        pltpu.VMEM((HEAD_DIM, t_ctx), BF16),
                        pltpu.VMEM((3, nq, NA_KROWS * GRID_W), F32)],
        compiler_params=_params("parallel"),
        name="natten",
    )(p_lat, p_lat, p_lat, p_ctx, p_ctx, rtab, ctab, cbias)


def _outproj_kernel(hg_ref, na_ref, h_ref, gate_ref, w_ref, o_ref):
    hw = hg_ref.shape[1]
    mix = _dot(hg_ref[...], w_ref[:hw, :]) + _dot(na_ref[...], w_ref[hw:, :])
    o_ref[...] = h_ref[...] + gate_ref[...] * mix


def _outproj_call(hg, na, h, gate, w, tm=512):
    t, d = h.shape
    hw, nw = hg.shape[1], na.shape[1]
    assert t % tm == 0
    return pl.pallas_call(
        _outproj_kernel,
        grid=(t // tm,),
        in_specs=[pl.BlockSpec((tm, hw), lambda i: (i, 0)),
                  pl.BlockSpec((tm, nw), lambda i: (i, 0)),
                  pl.BlockSpec((tm, d), lambda i: (i, 0)),
                  pl.BlockSpec((1, d), lambda i: (0, 0)),
                  pl.BlockSpec((hw + nw, d), lambda i: (0, 0))],
        out_specs=pl.BlockSpec((tm, d), lambda i: (i, 0)),
        out_shape=jax.ShapeDtypeStruct((t, d), F32),
        compiler_params=_params("parallel"),
        name="outproj",
    )(hg, na, h, gate, w)


def kernel(x, c, ctx, c_ctx, w_mod, b_mod, norm_w, ffn1_w_gate, ffn1_w_up, ffn1_w_down, w_in, hg_lb_logits,
           hg_norm_w, na_rpb, w_out, ffn2_w_gate, ffn2_w_up, ffn2_w_down, final_norm_w):
    batch, _, d = x.shape
    depth = w_mod.shape[0]
    assert batch == 1 and depth == 1
    hg_heads = hg_lb_logits.shape[-1] // HEAD_DIM
    na_heads = na_rpb.shape[1]
    row = lambda v: v.reshape(1, -1)

    cc = jnp.stack([c[0], c_ctx], axis=1)
    mods = _mod_call(cc, w_mod[0], row(b_mod[0]))
    m_l = mods[0].reshape(N_MOD, d)
    m_c = mods[1].reshape(N_MOD, d)
    lb = jnp.cumsum(jax.nn.softmax(hg_lb_logits.astype(F32), axis=1), axis=1)[:, 0]

    h_c, *ffn1 = _ffn_call(ctx[0], m_c[0:3], row(norm_w[0, 0]), ffn1_w_gate[0], ffn1_w_up[0], ffn1_w_down[0])
    h_l, w_in_b = _ffn_call(x[0], m_l[0:3], row(norm_w[0, 0]), *ffn1, side_casts=(w_in[0],))
    p_l = _inproj_call(h_l, m_l[3:5], row(norm_w[0, 1]), w_in_b)
    p_c = _inproj_call(h_c, m_c[3:5], row(norm_w[0, 1]), w_in_b)

    st0 = _hg_ctx_call(p_c, lb, n_heads=hg_heads)
    o_fw, *ffn2, w_out_b = _hg_call(p_l, st0[0], lb[0:1], None, None, rev=False, n_heads=hg_heads,
                                    side_casts=(ffn2_w_gate[0], ffn2_w_up[0], ffn2_w_down[0], w_out[0]))
    hg, = _hg_call(p_l, st0[1], lb[1:2], o_fw, row(hg_norm_w[0]), rev=True, n_heads=hg_heads)
    na = _na_call(p_c, p_l, na_rpb[0], col0=5 * hg_heads, n_heads=na_heads)

    h_l = _outproj_call(hg, na, h_l, m_l[5:6], w_out_b)
    out, = _ffn_call(h_l, m_l[6:9], row(norm_w[0, 2]), *ffn2, final_nw=row(final_norm_w))
    return out[None]
```

```python
import functools

import numpy as np
import jax
import jax.numpy as jnp
from jax import lax
from jax.experimental import pallas as pl
from jax.experimental.pallas import tpu as pltpu

F32 = jnp.float32
BF16 = jnp.bfloat16

EPS = 1e-6
GRID_W = 64
HEAD_DIM = 128
NA_KH = 8
NA_KW = 16
ROPE_BASE = 10000.0
N_MOD = 9
NEG_BIG = -1e30
LOG2E = 1.4426950408889634

V7X_VMEM_BYTES = 64 * 1024 * 1024
VMEM_LIMIT = V7X_VMEM_BYTES - 8 * 1024 * 1024

HG_CHUNK = 128
HG_LEVELS = (64, 32, 16, 8, 4, 2, 1)
NA_QROWS = 4
NA_KROWS = NA_QROWS + NA_KH
NA_STEP_BLOCKS = 4
NA_PREP_TILES = 4


def _dot(a, b):
    return jnp.dot(a, b, preferred_element_type=F32)


def _dot_nt(a, b):
    return lax.dot_general(a, b, (((1,), (1,)), ((), ())), preferred_element_type=F32)


def _dot_tn(a, b):
    return lax.dot_general(a, b, (((0,), (0,)), ((), ())), preferred_element_type=F32)


def _sigmoid(x):
    return 0.5 * jnp.tanh(0.5 * x) + 0.5


def _silu(x):
    return x * _sigmoid(x)


def _rmsnorm(x, w):
    return x * lax.rsqrt(jnp.mean(x * x, axis=-1, keepdims=True) + EPS) * w


def _params(*semantics):
    return pltpu.CompilerParams(dimension_semantics=semantics, vmem_limit_bytes=VMEM_LIMIT)


def _mod_kernel(c_ref, w_ref, b_ref, o_ref, sc_scr):
    d, tn = w_ref.shape
    n_vec = c_ref.shape[1]
    sub, lanes = 8, sc_scr.shape[-1]

    @pl.when(pl.program_id(0) == 0)
    def _():
        sc = _silu(c_ref[...])
        for r in range(n_vec):
            sc_scr[r] = jnp.broadcast_to(sc[:, r:r + 1], (d, lanes))

    def body(i, accs):
        rows = pl.ds(pl.multiple_of(i * sub, sub), sub)
        w = w_ref[rows, :]
        return tuple(acc + w * jnp.concatenate([sc_scr[r, rows, :]] * (tn // lanes), axis=1)
                     for r, acc in enumerate(accs))

    zero = jnp.zeros((sub, tn), F32)
    accs = lax.fori_loop(0, d // sub, body, (zero,) * n_vec, unroll=8)
    for r, acc in enumerate(accs):
        o_ref[r:r + 1, :] = jnp.sum(acc, axis=0, keepdims=True) + b_ref[...]


def _mod_call(cc, w_mod, b_mod):
    d, n_vec = cc.shape
    n = w_mod.shape[1]
    tn = next(t for t in (1024, 512, 256, 128) if n % t == 0)
    return pl.pallas_call(
        _mod_kernel,
        grid=(n // tn,),
        in_specs=[pl.BlockSpec((d, n_vec), lambda j: (0, 0)),
                  pl.BlockSpec((d, tn), lambda j: (0, j)),
                  pl.BlockSpec((1, tn), lambda j: (0, j))],
        out_specs=pl.BlockSpec((n_vec, tn), lambda j: (0, j)),
        out_shape=jax.ShapeDtypeStruct((n_vec, n), F32),
        scratch_shapes=[pltpu.VMEM((n_vec, d, HEAD_DIM), F32)],
        compiler_params=_params("arbitrary"),
        name="mod",
    )(cc, w_mod, b_mod)


def _adaln(x, nw, mod_ref):
    return (_rmsnorm(x, nw) * (1.0 + mod_ref[1:2, :]) + mod_ref[0:1, :]).astype(BF16)


def _ffn_kernel(*refs, nf, final_norm, emit_weights, n_side):
    refs = list(refs)
    x_ref, mod_ref, nw_ref, wg_ref, wu_ref, wd_ref = refs[:6]
    del refs[:6]
    fnw_ref = refs.pop(0) if final_norm else None
    side_in = [refs.pop(0) for _ in range(n_side)]
    o_ref = refs.pop(0)
    w_out = [refs.pop(0) for _ in range(3)] if emit_weights else None
    side_out = [refs.pop(0) for _ in range(n_side)]
    u_scr, acc_scr = refs
    f = pl.program_id(1)

    for src, dst in zip(side_in, side_out):
        dst[...] = src[...].astype(BF16)

    def tile_product(u):
        wg, wu, wd = wg_ref[...], wu_ref[...], wd_ref[...]
        if emit_weights:
            wg, wu, wd = wg.astype(BF16), wu.astype(BF16), wd.astype(BF16)
            for dst, w in zip(w_out, (wg, wu, wd)):
                dst[...] = w
        a = (_silu(_dot(u, wg)) * _dot(u, wu)).astype(BF16)
        return _dot(a, wd)

    @pl.when(f == 0)
    def _():
        u = _adaln(x_ref[...], nw_ref[...], mod_ref)
        u_scr[...] = u
        acc_scr[...] = tile_product(u)

    @pl.when((f > 0) & (f < nf - 1))
    def _():
        acc_scr[...] += tile_product(u_scr[...])

    @pl.when(f == nf - 1)
    def _():
        h = x_ref[...] + 0.5 * mod_ref[2:3, :] * (acc_scr[...] + tile_product(u_scr[...]))
        if final_norm:
            h = _rmsnorm(h, fnw_ref[...])
        o_ref[...] = h


def _side_cast_spec(shape, n_blk, nf):
    rows, cols = shape
    assert rows % (16 * n_blk) == 0
    n_col = next(c for c in range(nf, 0, -1) if cols % c == 0 and (cols // c) % HEAD_DIM == 0)
    return pl.BlockSpec((rows // n_blk, cols // n_col), lambda i, f: (i, jnp.minimum(f, n_col - 1)))


def _ffn_call(x, mod3, nw, wg, wu, wd, final_nw=None, side_casts=(), tm=512, tf=512):
    t, d = x.shape
    dff = wg.shape[1]
    tm = min(tm, t)
    nf, n_blk = dff // tf, t // tm
    assert t % tm == 0 and dff % tf == 0 and nf >= 2
    final_norm = final_nw is not None
    emit_weights = wg.dtype == F32
    assert not emit_weights or n_blk == 1
    w_specs = [pl.BlockSpec((d, tf), lambda i, f: (0, f)),
               pl.BlockSpec((d, tf), lambda i, f: (0, f)),
               pl.BlockSpec((tf, d), lambda i, f: (f, 0))]
    side_specs = [_side_cast_spec(a.shape, n_blk, nf) for a in side_casts]
    in_specs = [pl.BlockSpec((tm, d), lambda i, f: (i, 0)),
                pl.BlockSpec((3, d), lambda i, f: (0, 0)),
                pl.BlockSpec((1, d), lambda i, f: (0, 0))] + w_specs
    args = [x, mod3, nw, wg, wu, wd]
    if final_norm:
        in_specs.append(pl.BlockSpec((1, d), lambda i, f: (0, 0)))
        args.append(final_nw)
    in_specs += side_specs
    args += list(side_casts)
    out_specs = [pl.BlockSpec((tm, d), lambda i, f: (i, 0))]
    out_shape = [jax.ShapeDtypeStruct((t, d), F32)]
    if emit_weights:
        out_specs += w_specs
        out_shape += [jax.ShapeDtypeStruct(w.shape, BF16) for w in (wg, wu, wd)]
    out_specs += side_specs
    out_shape += [jax.ShapeDtypeStruct(a.shape, BF16) for a in side_casts]
    return pl.pallas_call(
        functools.partial(_ffn_kernel, nf=nf, final_norm=final_norm, emit_weights=emit_weights,
                          n_side=len(side_casts)),
        grid=(n_blk, nf),
        in_specs=in_specs,
        out_specs=out_specs,
        out_shape=out_shape,
        scratch_shapes=[pltpu.VMEM((tm, d), BF16), pltpu.VMEM((tm, d), F32)],
        compiler_params=_params("parallel", "arbitrary"),
        name="ffn",
    )(*args)


def _inproj_kernel(x_ref, mod_ref, nw_ref, w_ref, o_ref, u_scr):
    @pl.when(pl.program_id(1) == 0)
    def _():
        u = _adaln(x_ref[...], nw_ref[...], mod_ref)
        u_scr[...] = u
        o_ref[...] = _dot(u, w_ref[...])

    @pl.when(pl.program_id(1) > 0)
    def _():
        o_ref[...] = _dot(u_scr[...], w_ref[...])


def _inproj_call(x, mod2, nw, w, tm=1024, tn=1024):
    t, d = x.shape
    n = w.shape[1]
    tm = min(tm, t)
    assert t % tm == 0 and n % tn == 0
    return pl.pallas_call(
        _inproj_kernel,
        grid=(t // tm, n // tn),
        in_specs=[pl.BlockSpec((tm, d), lambda i, j: (i, 0)),
                  pl.BlockSpec((2, d), lambda i, j: (0, 0)),
                  pl.BlockSpec((1, d), lambda i, j: (0, 0)),
                  pl.BlockSpec((d, tn), lambda i, j: (0, j))],
        out_specs=pl.BlockSpec((tm, tn), lambda i, j: (i, j)),
        out_shape=jax.ShapeDtypeStruct((t, n), F32),
        scratch_shapes=[pltpu.VMEM((tm, d), BF16)],
        compiler_params=_params("parallel", "arbitrary"),
        name="inproj",
    )(x, mod2, nw, w)


def _split3(x):
    hi = x.astype(BF16)
    r = x - hi.astype(F32)
    mid = r.astype(BF16)
    lo = (r - mid.astype(F32)).astype(BF16)
    return hi, mid, lo


def _hg_gates(q_raw, f_raw, lb, tri):
    q = _silu(q_raw)
    f = lb + (1.0 - lb) * _sigmoid(f_raw)
    hi, mid, lo = _split3(jnp.log(f) * LOG2E)
    return q, 1.0 - f, f, _dot(tri, hi) + _dot(tri, mid) + _dot(tri, lo)


def _hg_level_operand(m, q, k, f, b, b_row, pos, rev):
    c = HG_CHUNK
    if m >= 8:
        qk, bmid = [], []
        for a in range(c // (2 * m)):
            lo_r, mid_r, hi_r = a * 2 * m, a * 2 * m + m, (a + 1) * 2 * m
            qk += [q[lo_r:mid_r], k[mid_r:hi_r]] if rev else [k[lo_r:mid_r], q[mid_r:hi_r]]
            bmid.append(jnp.broadcast_to(b_row(mid_r if rev else mid_r - 1), (2 * m, HEAD_DIM)))
        qk = jnp.concatenate(qk, axis=0)
        bmid = bmid[0] if len(bmid) == 1 else jnp.concatenate(bmid, axis=0)
        return (qk * jnp.exp2(-jnp.abs(b - bmid))).astype(BF16)
    half = pos & (2 * m - 1)
    q_side = (half < m) if rev else (half >= m)
    if m == 4:
        rows = [jnp.broadcast_to(b_row(a * 8 + (4 if rev else 3)), (8, HEAD_DIM)) for a in range(c // 8)]
        w = jnp.exp2(-jnp.abs(b - jnp.concatenate(rows, axis=0)))
    elif m == 2:
        f_next = pltpu.roll(f, c - 1, 0)
        f_prev = pltpu.roll(f, 1, 0)
        if rev:
            w = jnp.where(half == 0, f * f_next, jnp.where(half == 1, f, jnp.where(half == 2, 1.0, f_prev)))
        else:
            w = jnp.where(half == 0, f_next, jnp.where(half == 1, 1.0, jnp.where(half == 2, f, f * f_prev)))
    else:
        w = jnp.where(q_side, f, 1.0)
    return (jnp.where(q_side, q, k) * w).astype(BF16)


def _hg_state_step(st, k, vb, b, b_end):
    return st * jnp.exp2(b_end) + _dot_tn(vb, (k * jnp.exp2(b_end - b)).astype(BF16))


def _hg_ctx_kernel(q_ref, ff_ref, fb_ref, v_ref, lb_ref, tri_ref, o_ref, b_scr):
    c = HG_CHUNK
    n_chunks = q_ref.shape[0] // c
    for d, (f_ref, rev) in enumerate(((ff_ref, False), (fb_ref, True))):
        st = jnp.zeros((HEAD_DIM, HEAD_DIM), F32)
        for ci in (range(n_chunks - 1, -1, -1) if rev else range(n_chunks)):
            sl = pl.ds(ci * c, c)
            _, k, _, b = _hg_gates(q_ref[sl, :], f_ref[sl, :], lb_ref[d:d + 1, :], tri_ref[d])
            b_scr[...] = b
            st = _hg_state_step(st, k, v_ref[sl, :].astype(BF16), b, b_scr[pl.ds(0 if rev else c - 1, 1), :])
        o_ref[d] = st


def _hg_kernel(q_ref, f_ref, v_ref, st0_ref, lb_ref, tri_ref, lvl_ref, *rest, rev, heads_per_step, unroll,
               readout, n_side):
    rest = list(rest)
    of_ref, g_ref, nw_ref = (rest.pop(0), rest.pop(0), rest.pop(0)) if readout else (None, None, None)
    side_in = [rest.pop(0) for _ in range(n_side)]
    o_ref = rest.pop(0)
    side_out = [rest.pop(0) for _ in range(n_side)]
    b_scr, st_scr = rest
    c = HG_CHUNK

    for src, dst in zip(side_in, side_out):
        dst[...] = src[...].astype(BF16)

    @pl.when(pl.program_id(1) == 0)
    def _():
        st_scr[...] = st0_ref[...]

    n_chunks = q_ref.shape[0] // c
    tri = tri_ref[...]

    def body(i, carry):
        where = []
        for u in range(unroll):
            ci = i * unroll + u
            ci = (n_chunks - 1 - ci) if rev else ci
            rows = pl.ds(pl.multiple_of(ci * c, c), c)
            where += [(rows, slice(h * HEAD_DIM, (h + 1) * HEAD_DIM)) for h in range(heads_per_step)]
        n = len(where)
        gates = [_hg_gates(q_ref[r, cs], f_ref[r, cs], lb_ref[:, cs], tri) for r, cs in where]
        for p in range(n):
            b_scr[p] = gates[p][3]
        vbs = [v_ref[r, cs].astype(BF16) for r, cs in where]

        pos = lax.broadcasted_iota(jnp.int32, (c, HEAD_DIM), 0)
        lvl = lvl_ref[...]
        scores = [0.0] * n
        for m in HG_LEVELS:
            for p in range(n):
                q, k, f, b = gates[p]
                x = _hg_level_operand(m, q, k, f, b, lambda r, p=p: b_scr[p, pl.ds(r, 1), :], pos, rev)
                scores[p] = jnp.where(lvl == m, _dot_nt(x, x), scores[p])

        outs = [None] * n
        for h in range(heads_per_step):
            st = st_scr[h]
            for u in range(unroll):
                p = u * heads_per_step + h
                q, k, f, b = gates[p]
                outs[p] = _dot_nt((q * jnp.exp2(b)).astype(BF16), st.astype(BF16))
                st = _hg_state_step(st, k, vbs[p], b, b_scr[p, pl.ds(0 if rev else c - 1, 1), :])
            st_scr[h] = st

        for p, (r, cs) in enumerate(where):
            q, k, f, b = gates[p]
            o = outs[p] + _dot(scores[p].astype(BF16), vbs[p])
            o = o + jnp.sum(q * k, axis=-1, keepdims=True) * v_ref[r, cs]
            if readout:
                o = _rmsnorm(of_ref[r, cs] + o, nw_ref[...]) * _silu(g_ref[r, cs])
            o_ref[r, cs] = o.astype(o_ref.dtype)
        return carry

    lax.fori_loop(0, n_chunks // unroll, body, 0)


def _hg_tables(rev):
    t = np.arange(HG_CHUNK)[:, None]
    s = np.arange(HG_CHUNK)[None, :]
    x = t ^ s
    msb = np.where(x > 0, 2 ** np.floor(np.log2(np.maximum(x, 1))).astype(np.int64), 0)
    lvl = np.where((s > t) if rev else (s < t), msb, 0).astype(np.int32)
    tri = np.tril(np.ones((HG_CHUNK, HG_CHUNK), np.float32))
    return (tri.T if rev else tri), lvl


def _hg_ctx_call(p_ctx, lb, *, n_heads):
    t_ctx = p_ctx.shape[0]
    assert t_ctx % HG_CHUNK == 0
    tri = jnp.asarray(np.stack([_hg_tables(False)[0], _hg_tables(True)[0]]), BF16)
    col = lambda c0: pl.BlockSpec((t_ctx, HEAD_DIM), lambda h: (0, c0 + h))
    const = pl.BlockSpec((2, HG_CHUNK, HG_CHUNK), lambda h: (0, 0, 0))
    return pl.pallas_call(
        _hg_ctx_kernel,
        grid=(n_heads,),
        in_specs=[col(0), col(n_heads), col(2 * n_heads), col(3 * n_heads),
                  pl.BlockSpec((2, HEAD_DIM), lambda h: (0, h)), const],
        out_specs=pl.BlockSpec((2, None, HEAD_DIM, HEAD_DIM), lambda h: (0, h, 0, 0)),
        out_shape=jax.ShapeDtypeStruct((2, n_heads, HEAD_DIM, HEAD_DIM), F32),
        scratch_shapes=[pltpu.VMEM((HG_CHUNK, HEAD_DIM), F32)],
        compiler_params=_params("parallel"),
        name="hgrn_ctx",
    )(p_ctx, p_ctx, p_ctx, p_ctx, lb, tri)


def _hg_call(p_lat, st0, lb_dir, o_fw, norm_w, *, rev, n_heads, side_casts=(), tb=1024, heads_per_step=2,
             unroll=4):
    t_lat = p_lat.shape[0]
    tb = min(tb, t_lat)
    hb = heads_per_step
    assert t_lat % tb == 0 and tb % (HG_CHUNK * unroll) == 0 and n_heads % hb == 0
    n_blk, n_grp = t_lat // tb, n_heads // hb
    readout = o_fw is not None
    blk_row = (lambda j: n_blk - 1 - j) if rev else (lambda j: j)
    lat = lambda c0: pl.BlockSpec((tb, hb * HEAD_DIM), lambda g, j: (blk_row(j), c0 * n_grp + g))
    tri, lvl = _hg_tables(rev)
    const = pl.BlockSpec((HG_CHUNK, HG_CHUNK), lambda g, j: (0, 0))
    in_specs = [lat(0), lat(2 if rev else 1), lat(3),
                pl.BlockSpec((hb, HEAD_DIM, HEAD_DIM), lambda g, j: (g, 0, 0)),
                pl.BlockSpec((1, hb * HEAD_DIM), lambda g, j: (0, g)), const, const]
    args = [p_lat, p_lat, p_lat, st0, lb_dir, jnp.asarray(tri, BF16), jnp.asarray(lvl)]
    out_spec = pl.BlockSpec((tb, hb * HEAD_DIM), lambda g, j: (blk_row(j), g))
    if readout:
        in_specs += [out_spec, lat(4), pl.BlockSpec((1, HEAD_DIM), lambda g, j: (0, 0))]
        args += [o_fw, p_lat, norm_w]
    side_specs = [_side_cast_spec(a.shape, n_grp, n_blk) for a in side_casts]
    return pl.pallas_call(
        functools.partial(_hg_kernel, rev=rev, heads_per_step=hb, unroll=unroll, readout=readout,
                          n_side=len(side_casts)),
        grid=(n_grp, n_blk),
        in_specs=in_specs + side_specs,
        out_specs=[out_spec] + side_specs,
        out_shape=[jax.ShapeDtypeStruct((t_lat, n_heads * HEAD_DIM), BF16 if readout else F32)]
        + [jax.ShapeDtypeStruct(a.shape, BF16) for a in side_casts],
        scratch_shapes=[pltpu.VMEM((unroll * hb, HG_CHUNK, HEAD_DIM), F32),
                        pltpu.VMEM((hb, HEAD_DIM, HEAD_DIM), F32)],
        compiler_params=_params("parallel", "arbitrary"),
        name="hgrn_bwd" if rev else "hgrn_fwd",
    )(*args, *side_casts)


def _rope(x, cos, sin_signed, low):
    return x * cos + jnp.where(low, pltpu.roll(x, 96, 1), pltpu.roll(x, 32, 1)) * sin_signed


def _na_kernel(q_ref, k_ref, v_ref, kc_ref, vc_ref, rtab_ref, ctab_ref, cbias_ref,
               o_ref, kt_scr, v_scr, kct_scr, bias_scr, *, rows):
    tile = 2 * GRID_W
    n_rb = rows // NA_QROWS
    lane = lax.broadcasted_iota(jnp.int32, (GRID_W, HEAD_DIM), 1)
    low = (lane & (HEAD_DIM // 2 - 1)) < HEAD_DIM // 4

    def rope_rows(x_ref, row0, n):
        out = []
        for i in range(n):
            x = x_ref[pl.ds(pl.multiple_of((row0 + i) * GRID_W, GRID_W), GRID_W), :]
            cos = rtab_ref[0, pl.ds(row0 + i, 1), :] + ctab_ref[0]
            sin = rtab_ref[1, pl.ds(row0 + i, 1), :] + ctab_ref[1]
            out.append(_rope(x, cos, sin, low))
        return jnp.concatenate(out, axis=0)

    def prep(i, carry):
        js = [i * NA_PREP_TILES + u for u in range(NA_PREP_TILES)]
        roped = [rope_rows(k_ref, 2 * j, 2) for j in js]
        for j, kr in zip(js, roped):
            kt_scr[j] = kr.T.astype(BF16)
            v_scr[j] = v_ref[pl.ds(pl.multiple_of(j * tile, tile), tile), :].astype(BF16)
        return carry

    lax.fori_loop(0, rows // (2 * NA_PREP_TILES), prep, 0)
    kct_scr[...] = kc_ref[...].T.astype(BF16)

    bias_scr[...] = jnp.full(bias_scr.shape, NEG_BIG, F32)
    for e in range(3):
        for a in range(NA_QROWS):
            b0 = (0, a, NA_KROWS - NA_KH)[e]
            d0 = (NA_KH - 1 - a, NA_KH // 2 - 1, NA_QROWS - 1 - a)[e]
            for i in range(NA_KH):
                bias_scr[e, a * GRID_W:(a + 1) * GRID_W, (b0 + i) * GRID_W:(b0 + i + 1) * GRID_W] = (
                    cbias_ref[d0 + i] * LOG2E)

    vc = vc_ref[...].astype(BF16)

    def row_blocks(i, carry):
        rbs = [i * NA_STEP_BLOCKS + u for u in range(NA_STEP_BLOCKS)]
        qs = [(rope_rows(q_ref, rb * NA_QROWS, NA_QROWS) * (HEAD_DIM ** -0.5 * LOG2E)).astype(BF16) for rb in rbs]
        j0s = [jnp.clip(rb * (NA_QROWS // 2) - NA_KH // 4, 0, (rows - NA_KROWS) // 2) for rb in rbs]
        s_w, s_c = [], []
        for rb, q, j0 in zip(rbs, qs, j0s):
            kw = jnp.concatenate([kt_scr[j0 + t] for t in range(NA_KROWS // 2)], axis=1)
            edge = jnp.where(rb == 0, 0, jnp.where(rb == n_rb - 1, 2, 1))
            s_w.append(_dot(q, kw) + bias_scr[edge])
            s_c.append(_dot(q, kct_scr[...]))
        nq = NA_QROWS * GRID_W
        for rb, j0, sw, sc in zip(rbs, j0s, s_w, s_c):
            m = jnp.maximum(jnp.max(sw, axis=-1, keepdims=True), jnp.max(sc, axis=-1, keepdims=True))
            p_w = jnp.exp2(sw - m)
            p_c = jnp.exp2(sc - m)
            denom = jnp.sum(p_w, axis=-1, keepdims=True) + jnp.sum(p_c, axis=-1, keepdims=True)
            vw = jnp.concatenate([v_scr[j0 + t] for t in range(NA_KROWS // 2)], axis=0)
            o = _dot(p_w.astype(BF16), vw) + _dot(p_c.astype(BF16), vc)
            o_ref[pl.ds(pl.multiple_of(rb * nq, nq), nq), :] = (o / denom).astype(o_ref.dtype)
        return carry

    lax.fori_loop(0, n_rb // NA_STEP_BLOCKS, row_blocks, 0)


def _na_col_bias(rpb):
    col = np.arange(GRID_W)
    col_start = np.clip(col - NA_KW // 2, 0, GRID_W - NA_KW)
    col_in = (col[None, :] >= col_start[:, None]) & (col[None, :] < col_start[:, None] + NA_KW)
    dc = np.clip(col[None, :] - col[:, None], 1 - NA_KW, NA_KW - 1) + (NA_KW - 1)
    onehot = (dc[None] == np.arange(2 * NA_KW - 1)[:, None, None]).astype(np.float32)
    vals = jnp.einsum("hrk,kqc->hrqc", rpb.astype(F32), onehot, precision=lax.Precision.HIGHEST)
    return jnp.where(col_in[None, None], vals, NEG_BIG)


def _rope_tables(rows):
    quarter = HEAD_DIM // 4
    inv = ROPE_BASE ** (-jnp.arange(quarter, dtype=F32) / quarter)
    ang_r = jnp.arange(rows).astype(F32)[:, None] * inv[None, :]
    ang_c = jnp.arange(GRID_W).astype(F32)[:, None] * inv[None, :]
    half = lambda a: jnp.stack([jnp.concatenate([jnp.cos(a), jnp.cos(a)], axis=-1),
                                jnp.concatenate([-jnp.sin(a), jnp.sin(a)], axis=-1)])
    r, c = half(ang_r), half(ang_c)
    return jnp.concatenate([r, jnp.zeros_like(r)], axis=-1), jnp.concatenate([jnp.zeros_like(c), c], axis=-1)


def _na_call(p_ctx, p_lat, rpb, *, col0, n_heads):
    t_ctx, t = p_ctx.shape[0], p_lat.shape[0]
    rows = t // GRID_W
    assert t % GRID_W == 0 and rows % (NA_QROWS * NA_STEP_BLOCKS) == 0 and rows >= NA_KROWS + NA_QROWS
    assert rows % (2 * NA_PREP_TILES) == 0
    assert NA_QROWS % 2 == 0 and NA_KH % 4 == 0
    nq = NA_QROWS * GRID_W
    rtab, ctab = _rope_tables(rows)
    cbias = _na_col_bias(rpb)
    full = lambda col: pl.BlockSpec((t, HEAD_DIM), lambda h: (0, col + h))
    ctx = lambda col: pl.BlockSpec((t_ctx, HEAD_DIM), lambda h: (0, col + h))
    return pl.pallas_call(
        functools.partial(_na_kernel, rows=rows),
        grid=(n_heads,),
        in_specs=[full(col0), full(col0 + n_heads), full(col0 + 2 * n_heads),
                  ctx(col0 + n_heads), ctx(col0 + 2 * n_heads),
                  pl.BlockSpec((2, rows, HEAD_DIM), lambda h: (0, 0, 0)),
                  pl.BlockSpec((2, GRID_W, HEAD_DIM), lambda h: (0, 0, 0)),
                  pl.BlockSpec((None, 2 * NA_KH - 1, GRID_W, GRID_W), lambda h: (h, 0, 0, 0))],
        out_specs=pl.BlockSpec((t, HEAD_DIM), lambda h: (0, h)),
        out_shape=jax.ShapeDtypeStruct((t, n_heads * HEAD_DIM), BF16),
        scratch_shapes=[pltpu.VMEM((rows // 2, HEAD_DIM, 2 * GRID_W), BF16),
                        pltpu.VMEM((rows // 2, 2 * GRID_W, HEAD_DIM), BF16),
                        pltpu.VMEM((HEAD_DIM, t_ctx), BF16),
                        pltpu.VMEM((3, nq, NA_KROWS * GRID_W), F32)],
        compiler_params=_params("parallel"),
        name="natten",
    )(p_lat, p_lat, p_lat, p_ctx, p_ctx, rtab, ctab, cbias)


def _outproj_kernel(hg_ref, na_ref, h_ref, gate_ref, w_ref, o_ref):
    hw = hg_ref.shape[1]
    mix = _dot(hg_ref[...], w_ref[:hw, :]) + _dot(na_ref[...], w_ref[hw:, :])
    o_ref[...] = h_ref[...] + gate_ref[...] * mix


def _outproj_call(hg, na, h, gate, w, tm=512):
    t, d = h.shape
    hw, nw = hg.shape[1], na.shape[1]
    assert t % tm == 0
    return pl.pallas_call(
        _outproj_kernel,
        grid=(t // tm,),
        in_specs=[pl.BlockSpec((tm, hw), lambda i: (i, 0)),
                  pl.BlockSpec((tm, nw), lambda i: (i, 0)),
                  pl.BlockSpec((tm, d), lambda i: (i, 0)),
                  pl.BlockSpec((1, d), lambda i: (0, 0)),
                  pl.BlockSpec((hw + nw, d), lambda i: (0, 0))],
        out_specs=pl.BlockSpec((tm, d), lambda i: (i, 0)),
        out_shape=jax.ShapeDtypeStruct((t, d), F32),
        compiler_params=_params("parallel"),
        name="outproj",
    )(hg, na, h, gate, w)


def kernel(x, c, ctx, c_ctx, w_mod, b_mod, norm_w, ffn1_w_gate, ffn1_w_up, ffn1_w_down, w_in, hg_lb_logits,
           hg_norm_w, na_rpb, w_out, ffn2_w_gate, ffn2_w_up, ffn2_w_down, final_norm_w):
    batch, _, d = x.shape
    depth = w_mod.shape[0]
    assert batch == 1 and depth == 1
    hg_heads = hg_lb_logits.shape[-1] // HEAD_DIM
    na_heads = na_rpb.shape[1]
    row = lambda v: v.reshape(1, -1)

    cc = jnp.stack([c[0], c_ctx], axis=1)
    mods = _mod_call(cc, w_mod[0], row(b_mod[0]))
    m_l = mods[0].reshape(N_MOD, d)
    m_c = mods[1].reshape(N_MOD, d)
    lb = jnp.cumsum(jax.nn.softmax(hg_lb_logits.astype(F32), axis=1), axis=1)[:, 0]

    h_c, *ffn1 = _ffn_call(ctx[0], m_c[0:3], row(norm_w[0, 0]), ffn1_w_gate[0], ffn1_w_up[0], ffn1_w_down[0])
    h_l, w_in_b = _ffn_call(x[0], m_l[0:3], row(norm_w[0, 0]), *ffn1, side_casts=(w_in[0],))
    p_l = _inproj_call(h_l, m_l[3:5], row(norm_w[0, 1]), w_in_b)
    p_c = _inproj_call(h_c, m_c[3:5], row(norm_w[0, 1]), w_in_b)

    st0 = _hg_ctx_call(p_c, lb, n_heads=hg_heads)
    o_fw, wg2, w_out_b = _hg_call(p_l, st0[0], lb[0:1], None, None, rev=False, n_heads=hg_heads,
                                  side_casts=(ffn2_w_gate[0], w_out[0]))
    hg, wu2, wd2 = _hg_call(p_l, st0[1], lb[1:2], o_fw, row(hg_norm_w[0]), rev=True, n_heads=hg_heads,
                            side_casts=(ffn2_w_up[0], ffn2_w_down[0]))
    ffn2 = (wg2, wu2, wd2)
    na = _na_call(p_c, p_l, na_rpb[0], col0=5 * hg_heads, n_heads=na_heads)

    h_l = _outproj_call(hg, na, h_l, m_l[5:6], w_out_b)
    out, = _ffn_call(h_l, m_l[6:9], row(norm_w[0, 2]), *ffn2, final_nw=row(final_norm_w))
    return out[None]
```

```python
import functools

import numpy as np
import jax
import jax.numpy as jnp
from jax import lax
from jax.experimental import pallas as pl
from jax.experimental.pallas import tpu as pltpu

F32 = jnp.float32
BF16 = jnp.bfloat16

EPS = 1e-6
GRID_W = 64
HEAD_DIM = 128
NA_KH = 8
NA_KW = 16
ROPE_BASE = 10000.0
N_MOD = 9
NEG_BIG = -1e30
LOG2E = 1.4426950408889634

V7X_VMEM_BYTES = 64 * 1024 * 1024
VMEM_LIMIT = V7X_VMEM_BYTES - 8 * 1024 * 1024

HG_CHUNK = 128
HG_LEVELS = (64, 32, 16, 8, 4, 2, 1)
NA_QROWS = 4
NA_KROWS = NA_QROWS + NA_KH
NA_STEP_BLOCKS = 4
NA_PREP_TILES = 8


def _dot(a, b):
    return jnp.dot(a, b, preferred_element_type=F32)


def _dot_nt(a, b):
    return lax.dot_general(a, b, (((1,), (1,)), ((), ())), preferred_element_type=F32)


def _dot_tn(a, b):
    return lax.dot_general(a, b, (((0,), (0,)), ((), ())), preferred_element_type=F32)


def _sigmoid(x):
    return 0.5 * jnp.tanh(0.5 * x) + 0.5


def _silu(x):
    return x * _sigmoid(x)


def _rmsnorm(x, w):
    return x * lax.rsqrt(jnp.mean(x * x, axis=-1, keepdims=True) + EPS) * w


def _params(*semantics):
    return pltpu.CompilerParams(dimension_semantics=semantics, vmem_limit_bytes=VMEM_LIMIT)


def _mod_kernel(c_ref, w_ref, b_ref, o_ref, sc_scr):
    d, tn = w_ref.shape
    n_vec = c_ref.shape[1]
    sub, lanes = 8, sc_scr.shape[-1]

    @pl.when(pl.program_id(0) == 0)
    def _():
        sc = _silu(c_ref[...])
        for r in range(n_vec):
            sc_scr[r] = jnp.broadcast_to(sc[:, r:r + 1], (d, lanes))

    def body(i, accs):
        rows = pl.ds(pl.multiple_of(i * sub, sub), sub)
        w = w_ref[rows, :]
        return tuple(acc + w * jnp.concatenate([sc_scr[r, rows, :]] * (tn // lanes), axis=1)
                     for r, acc in enumerate(accs))

    zero = jnp.zeros((sub, tn), F32)
    accs = lax.fori_loop(0, d // sub, body, (zero,) * n_vec, unroll=8)
    for r, acc in enumerate(accs):
        o_ref[r:r + 1, :] = jnp.sum(acc, axis=0, keepdims=True) + b_ref[...]


def _mod_call(cc, w_mod, b_mod):
    d, n_vec = cc.shape
    n = w_mod.shape[1]
    tn = next(t for t in (2048, 1024, 512, 256, 128) if n % t == 0)
    return pl.pallas_call(
        _mod_kernel,
        grid=(n // tn,),
        in_specs=[pl.BlockSpec((d, n_vec), lambda j: (0, 0)),
                  pl.BlockSpec((d, tn), lambda j: (0, j)),
                  pl.BlockSpec((1, tn), lambda j: (0, j))],
        out_specs=pl.BlockSpec((n_vec, tn), lambda j: (0, j)),
        out_shape=jax.ShapeDtypeStruct((n_vec, n), F32),
        scratch_shapes=[pltpu.VMEM((n_vec, d, HEAD_DIM), F32)],
        compiler_params=_params("arbitrary"),
        name="mod",
    )(cc, w_mod, b_mod)


def _adaln(x, nw, mod_ref):
    return (_rmsnorm(x, nw) * (1.0 + mod_ref[1:2, :]) + mod_ref[0:1, :]).astype(BF16)


def _ffn_kernel(*refs, nf, final_norm, emit_weights, n_side):
    refs = list(refs)
    x_ref, mod_ref, nw_ref, wg_ref, wu_ref, wd_ref = refs[:6]
    del refs[:6]
    fnw_ref = refs.pop(0) if final_norm else None
    side_in = [refs.pop(0) for _ in range(n_side)]
    o_ref = refs.pop(0)
    w_out = [refs.pop(0) for _ in range(3)] if emit_weights else None
    side_out = [refs.pop(0) for _ in range(n_side)]
    u_scr, x_scr, acc_scr = refs
    f = pl.program_id(1)

    for src, dst in zip(side_in, side_out):
        dst[...] = src[...].astype(BF16)

    def tile_product(u):
        wg, wu, wd = wg_ref[...], wu_ref[...], wd_ref[...]
        if emit_weights:
            wg, wu, wd = wg.astype(BF16), wu.astype(BF16), wd.astype(BF16)
            for dst, w in zip(w_out, (wg, wu, wd)):
                dst[...] = w
        a = (_silu(_dot(u, wg)) * _dot(u, wu)).astype(BF16)
        return _dot(a, wd)

    @pl.when(f == 0)
    def _():
        x = x_ref[...]
        x_scr[...] = x
        u = _adaln(x, nw_ref[...], mod_ref)
        u_scr[...] = u
        acc_scr[...] = tile_product(u)

    @pl.when((f > 0) & (f < nf - 1))
    def _():
        acc_scr[...] += tile_product(u_scr[...])

    @pl.when(f == nf - 1)
    def _():
        h = x_scr[...] + 0.5 * mod_ref[2:3, :] * (acc_scr[...] + tile_product(u_scr[...]))
        if final_norm:
            h = _rmsnorm(h, fnw_ref[...])
        o_ref[...] = h


def _side_cast_spec(shape, n_blk, nf):
    rows, cols = shape
    assert rows % (16 * n_blk) == 0
    n_col = next(c for c in range(nf, 0, -1) if cols % c == 0 and (cols // c) % HEAD_DIM == 0)
    return pl.BlockSpec((rows // n_blk, cols // n_col), lambda i, f: (i, jnp.minimum(f, n_col - 1)))


def _ffn_call(x, mod3, nw, wg, wu, wd, final_nw=None, side_casts=(), tm=512, tf=512):
    t, d = x.shape
    dff = wg.shape[1]
    tm = min(tm, t)
    nf, n_blk = dff // tf, t // tm
    assert t % tm == 0 and dff % tf == 0 and nf >= 2
    final_norm = final_nw is not None
    emit_weights = wg.dtype == F32
    assert not emit_weights or n_blk == 1
    w_specs = [pl.BlockSpec((d, tf), lambda i, f: (0, f)),
               pl.BlockSpec((d, tf), lambda i, f: (0, f)),
               pl.BlockSpec((tf, d), lambda i, f: (f, 0))]
    side_specs = [_side_cast_spec(a.shape, n_blk, nf) for a in side_casts]
    x_row = lambda i, f: jnp.where(f < nf // 2, i, jnp.minimum(i + 1, n_blk - 1))
    in_specs = [pl.BlockSpec((tm, d), lambda i, f: (x_row(i, f), 0)),
                pl.BlockSpec((3, d), lambda i, f: (0, 0)),
                pl.BlockSpec((1, d), lambda i, f: (0, 0))] + w_specs
    args = [x, mod3, nw, wg, wu, wd]
    if final_norm:
        in_specs.append(pl.BlockSpec((1, d), lambda i, f: (0, 0)))
        args.append(final_nw)
    in_specs += side_specs
    args += list(side_casts)
    out_specs = [pl.BlockSpec((tm, d), lambda i, f: (i, 0))]
    out_shape = [jax.ShapeDtypeStruct((t, d), F32)]
    if emit_weights:
        out_specs += w_specs
        out_shape += [jax.ShapeDtypeStruct(w.shape, BF16) for w in (wg, wu, wd)]
    out_specs += side_specs
    out_shape += [jax.ShapeDtypeStruct(a.shape, BF16) for a in side_casts]
    return pl.pallas_call(
        functools.partial(_ffn_kernel, nf=nf, final_norm=final_norm, emit_weights=emit_weights,
                          n_side=len(side_casts)),
        grid=(n_blk, nf),
        in_specs=in_specs,
        out_specs=out_specs,
        out_shape=out_shape,
        scratch_shapes=[pltpu.VMEM((tm, d), BF16), pltpu.VMEM((tm, d), F32), pltpu.VMEM((tm, d), F32)],
        compiler_params=_params("parallel", "arbitrary"),
        name="ffn",
    )(*args)


def _inproj_kernel(x_ref, mod_ref, nw_ref, w_ref, o_ref, u_scr):
    @pl.when(pl.program_id(1) == 0)
    def _():
        u = _adaln(x_ref[...], nw_ref[...], mod_ref)
        u_scr[...] = u
        o_ref[...] = _dot(u, w_ref[...])

    @pl.when(pl.program_id(1) > 0)
    def _():
        o_ref[...] = _dot(u_scr[...], w_ref[...])


def _inproj_call(x, mod2, nw, w, tm=1024, tn=1024):
    t, d = x.shape
    n = w.shape[1]
    tm = min(tm, t)
    assert t % tm == 0 and n % tn == 0
    return pl.pallas_call(
        _inproj_kernel,
        grid=(t // tm, n // tn),
        in_specs=[pl.BlockSpec((tm, d), lambda i, j: (i, 0)),
                  pl.BlockSpec((2, d), lambda i, j: (0, 0)),
                  pl.BlockSpec((1, d), lambda i, j: (0, 0)),
                  pl.BlockSpec((d, tn), lambda i, j: (0, j))],
        out_specs=pl.BlockSpec((tm, tn), lambda i, j: (i, j)),
        out_shape=jax.ShapeDtypeStruct((t, n), F32),
        scratch_shapes=[pltpu.VMEM((tm, d), BF16)],
        compiler_params=_params("parallel", "arbitrary"),
        name="inproj",
    )(x, mod2, nw, w)


def _split3(x):
    hi = x.astype(BF16)
    r = x - hi.astype(F32)
    mid = r.astype(BF16)
    lo = (r - mid.astype(F32)).astype(BF16)
    return hi, mid, lo


def _hg_gates(q_raw, f_raw, lb, tri):
    q = _silu(q_raw)
    f = lb + (1.0 - lb) * _sigmoid(f_raw)
    hi, mid, lo = _split3(jnp.log(f) * LOG2E)
    return q, 1.0 - f, f, _dot(tri, hi) + _dot(tri, mid) + _dot(tri, lo)


def _hg_level_operand(m, q, k, f, b, b_row, pos, rev):
    c = HG_CHUNK
    if m >= 8:
        qk, bmid = [], []
        for a in range(c // (2 * m)):
            lo_r, mid_r, hi_r = a * 2 * m, a * 2 * m + m, (a + 1) * 2 * m
            qk += [q[lo_r:mid_r], k[mid_r:hi_r]] if rev else [k[lo_r:mid_r], q[mid_r:hi_r]]
            bmid.append(jnp.broadcast_to(b_row(mid_r if rev else mid_r - 1), (2 * m, HEAD_DIM)))
        qk = jnp.concatenate(qk, axis=0)
        bmid = bmid[0] if len(bmid) == 1 else jnp.concatenate(bmid, axis=0)
        return (qk * jnp.exp2(-jnp.abs(b - bmid))).astype(BF16)
    half = pos & (2 * m - 1)
    q_side = (half < m) if rev else (half >= m)
    if m == 4:
        rows = [jnp.broadcast_to(b_row(a * 8 + (4 if rev else 3)), (8, HEAD_DIM)) for a in range(c // 8)]
        w = jnp.exp2(-jnp.abs(b - jnp.concatenate(rows, axis=0)))
    elif m == 2:
        f_next = pltpu.roll(f, c - 1, 0)
        f_prev = pltpu.roll(f, 1, 0)
        if rev:
            w = jnp.where(half == 0, f * f_next, jnp.where(half == 1, f, jnp.where(half == 2, 1.0, f_prev)))
        else:
            w = jnp.where(half == 0, f_next, jnp.where(half == 1, 1.0, jnp.where(half == 2, f, f * f_prev)))
    else:
        w = jnp.where(q_side, f, 1.0)
    return (jnp.where(q_side, q, k) * w).astype(BF16)


def _hg_state_step(st, k, vb, b, b_end):
    return st * jnp.exp2(b_end) + _dot_tn(vb, (k * jnp.exp2(b_end - b)).astype(BF16))


def _hg_ctx_kernel(q_ref, ff_ref, fb_ref, v_ref, lb_ref, tri_ref, o_ref, b_scr):
    c = HG_CHUNK
    n_chunks = q_ref.shape[0] // c
    for d, (f_ref, rev) in enumerate(((ff_ref, False), (fb_ref, True))):
        st = jnp.zeros((HEAD_DIM, HEAD_DIM), F32)
        for ci in (range(n_chunks - 1, -1, -1) if rev else range(n_chunks)):
            sl = pl.ds(ci * c, c)
            _, k, _, b = _hg_gates(q_ref[sl, :], f_ref[sl, :], lb_ref[d:d + 1, :], tri_ref[d])
            b_scr[...] = b
            st = _hg_state_step(st, k, v_ref[sl, :].astype(BF16), b, b_scr[pl.ds(0 if rev else c - 1, 1), :])
        o_ref[d] = st


def _hg_kernel(q_ref, f_ref, v_ref, st0_ref, lb_ref, tri_ref, lvl_ref, *rest, rev, heads_per_step, unroll,
               readout, n_side):
    rest = list(rest)
    of_ref, g_ref, nw_ref = (rest.pop(0), rest.pop(0), rest.pop(0)) if readout else (None, None, None)
    side_in = [rest.pop(0) for _ in range(n_side)]
    o_ref = rest.pop(0)
    side_out = [rest.pop(0) for _ in range(n_side)]
    b_scr, st_scr = rest
    c = HG_CHUNK

    for src, dst in zip(side_in, side_out):
        dst[...] = src[...].astype(BF16)

    @pl.when(pl.program_id(1) == 0)
    def _():
        st_scr[...] = st0_ref[...]

    n_chunks = q_ref.shape[0] // c
    tri = tri_ref[...]

    def body(i, carry):
        where = []
        for u in range(unroll):
            ci = i * unroll + u
            ci = (n_chunks - 1 - ci) if rev else ci
            rows = pl.ds(pl.multiple_of(ci * c, c), c)
            where += [(rows, slice(h * HEAD_DIM, (h + 1) * HEAD_DIM)) for h in range(heads_per_step)]
        n = len(where)
        gates = [_hg_gates(q_ref[r, cs], f_ref[r, cs], lb_ref[:, cs], tri) for r, cs in where]
        for p in range(n):
            b_scr[p] = gates[p][3]
        vbs = [v_ref[r, cs].astype(BF16) for r, cs in where]

        pos = lax.broadcasted_iota(jnp.int32, (c, HEAD_DIM), 0)
        lvl = lvl_ref[...]
        scores = [0.0] * n
        for m in HG_LEVELS:
            for p in range(n):
                q, k, f, b = gates[p]
                x = _hg_level_operand(m, q, k, f, b, lambda r, p=p: b_scr[p, pl.ds(r, 1), :], pos, rev)
                scores[p] = jnp.where(lvl == m, _dot_nt(x, x), scores[p])

        outs = [None] * n
        for h in range(heads_per_step):
            st = st_scr[h]
            for u in range(unroll):
                p = u * heads_per_step + h
                q, k, f, b = gates[p]
                outs[p] = _dot_nt((q * jnp.exp2(b)).astype(BF16), st.astype(BF16))
                st = _hg_state_step(st, k, vbs[p], b, b_scr[p, pl.ds(0 if rev else c - 1, 1), :])
            st_scr[h] = st

        for p, (r, cs) in enumerate(where):
            q, k, f, b = gates[p]
            o = outs[p] + _dot(scores[p].astype(BF16), vbs[p])
            o = o + jnp.sum(q * k, axis=-1, keepdims=True) * v_ref[r, cs]
            if readout:
                o = _rmsnorm(of_ref[r, cs] + o, nw_ref[...]) * _silu(g_ref[r, cs])
            o_ref[r, cs] = o.astype(o_ref.dtype)
        return carry

    lax.fori_loop(0, n_chunks // unroll, body, 0)


def _hg_tables(rev):
    t = np.arange(HG_CHUNK)[:, None]
    s = np.arange(HG_CHUNK)[None, :]
    x = t ^ s
    msb = np.where(x > 0, 2 ** np.floor(np.log2(np.maximum(x, 1))).astype(np.int64), 0)
    lvl = np.where((s > t) if rev else (s < t), msb, 0).astype(np.int32)
    tri = np.tril(np.ones((HG_CHUNK, HG_CHUNK), np.float32))
    return (tri.T if rev else tri), lvl


def _hg_ctx_call(p_ctx, lb, *, n_heads):
    t_ctx = p_ctx.shape[0]
    assert t_ctx % HG_CHUNK == 0
    tri = jnp.asarray(np.stack([_hg_tables(False)[0], _hg_tables(True)[0]]), BF16)
    col = lambda c0: pl.BlockSpec((t_ctx, HEAD_DIM), lambda h: (0, c0 + h))
    const = pl.BlockSpec((2, HG_CHUNK, HG_CHUNK), lambda h: (0, 0, 0))
    return pl.pallas_call(
        _hg_ctx_kernel,
        grid=(n_heads,),
        in_specs=[col(0), col(n_heads), col(2 * n_heads), col(3 * n_heads),
                  pl.BlockSpec((2, HEAD_DIM), lambda h: (0, h)), const],
        out_specs=pl.BlockSpec((2, None, HEAD_DIM, HEAD_DIM), lambda h: (0, h, 0, 0)),
        out_shape=jax.ShapeDtypeStruct((2, n_heads, HEAD_DIM, HEAD_DIM), F32),
        scratch_shapes=[pltpu.VMEM((HG_CHUNK, HEAD_DIM), F32)],
        compiler_params=_params("parallel"),
        name="hgrn_ctx",
    )(p_ctx, p_ctx, p_ctx, p_ctx, lb, tri)


def _hg_call(p_lat, st0, lb_dir, o_fw, norm_w, *, rev, n_heads, side_casts=(), tb=1024, heads_per_step=2,
             unroll=4):
    t_lat = p_lat.shape[0]
    tb = min(tb, t_lat)
    hb = heads_per_step
    assert t_lat % tb == 0 and tb % (HG_CHUNK * unroll) == 0 and n_heads % hb == 0
    n_blk, n_grp = t_lat // tb, n_heads // hb
    readout = o_fw is not None
    blk_row = (lambda j: n_blk - 1 - j) if rev else (lambda j: j)
    lat = lambda c0: pl.BlockSpec((tb, hb * HEAD_DIM), lambda g, j: (blk_row(j), c0 * n_grp + g))
    tri, lvl = _hg_tables(rev)
    const = pl.BlockSpec((HG_CHUNK, HG_CHUNK), lambda g, j: (0, 0))
    in_specs = [lat(0), lat(2 if rev else 1), lat(3),
                pl.BlockSpec((hb, HEAD_DIM, HEAD_DIM), lambda g, j: (g, 0, 0)),
                pl.BlockSpec((1, hb * HEAD_DIM), lambda g, j: (0, g)), const, const]
    args = [p_lat, p_lat, p_lat, st0, lb_dir, jnp.asarray(tri, BF16), jnp.asarray(lvl)]
    out_spec = pl.BlockSpec((tb, hb * HEAD_DIM), lambda g, j: (blk_row(j), g))
    if readout:
        in_specs += [out_spec, lat(4), pl.BlockSpec((1, HEAD_DIM), lambda g, j: (0, 0))]
        args += [o_fw, p_lat, norm_w]
    side_specs = [_side_cast_spec(a.shape, n_grp, n_blk) for a in side_casts]
    return pl.pallas_call(
        functools.partial(_hg_kernel, rev=rev, heads_per_step=hb, unroll=unroll, readout=readout,
                          n_side=len(side_casts)),
        grid=(n_grp, n_blk),
        in_specs=in_specs + side_specs,
        out_specs=[out_spec] + side_specs,
        out_shape=[jax.ShapeDtypeStruct((t_lat, n_heads * HEAD_DIM), BF16 if readout else F32)]
        + [jax.ShapeDtypeStruct(a.shape, BF16) for a in side_casts],
        scratch_shapes=[pltpu.VMEM((unroll * hb, HG_CHUNK, HEAD_DIM), F32),
                        pltpu.VMEM((hb, HEAD_DIM, HEAD_DIM), F32)],
        compiler_params=_params("parallel", "arbitrary"),
        name="hgrn_bwd" if rev else "hgrn_fwd",
    )(*args, *side_casts)


def _rope(x, cos, sin_signed, low):
    return x * cos + jnp.where(low, pltpu.roll(x, 96, 1), pltpu.roll(x, 32, 1)) * sin_signed


def _na_kernel(q_ref, k_ref, v_ref, kc_ref, vc_ref, rtab_ref, ctab_ref, cbias_ref,
               o_ref, kt_scr, v_scr, kct_scr, bias_scr, *, rows):
    tile = 2 * GRID_W
    n_rb = rows // NA_QROWS
    lane = lax.broadcasted_iota(jnp.int32, (GRID_W, HEAD_DIM), 1)
    low = (lane & (HEAD_DIM // 2 - 1)) < HEAD_DIM // 4

    def rope_rows(x_ref, row0, n):
        out = []
        for i in range(n):
            x = x_ref[pl.ds(pl.multiple_of((row0 + i) * GRID_W, GRID_W), GRID_W), :]
            cos = rtab_ref[0, pl.ds(row0 + i, 1), :] + ctab_ref[0]
            sin = rtab_ref[1, pl.ds(row0 + i, 1), :] + ctab_ref[1]
            out.append(_rope(x, cos, sin, low))
        return jnp.concatenate(out, axis=0)

    def prep(i, carry):
        js = [i * NA_PREP_TILES + u for u in range(NA_PREP_TILES)]
        roped = [rope_rows(k_ref, 2 * j, 2) for j in js]
        for j, kr in zip(js, roped):
            kt_scr[j] = kr.T.astype(BF16)
            v_scr[j] = v_ref[pl.ds(pl.multiple_of(j * tile, tile), tile), :].astype(BF16)
        return carry

    lax.fori_loop(0, rows // (2 * NA_PREP_TILES), prep, 0)
    kct_scr[...] = kc_ref[...].T.astype(BF16)

    bias_scr[...] = jnp.full(bias_scr.shape, NEG_BIG, F32)
    for e in range(3):
        for a in range(NA_QROWS):
            b0 = (0, a, NA_KROWS - NA_KH)[e]
            d0 = (NA_KH - 1 - a, NA_KH // 2 - 1, NA_QROWS - 1 - a)[e]
            for i in range(NA_KH):
                bias_scr[e, a * GRID_W:(a + 1) * GRID_W, (b0 + i) * GRID_W:(b0 + i + 1) * GRID_W] = (
                    cbias_ref[d0 + i] * LOG2E)

    vc = vc_ref[...].astype(BF16)

    def row_blocks(i, carry):
        rbs = [i * NA_STEP_BLOCKS + u for u in range(NA_STEP_BLOCKS)]
        qs = [(rope_rows(q_ref, rb * NA_QROWS, NA_QROWS) * (HEAD_DIM ** -0.5 * LOG2E)).astype(BF16) for rb in rbs]
        j0s = [jnp.clip(rb * (NA_QROWS // 2) - NA_KH // 4, 0, (rows - NA_KROWS) // 2) for rb in rbs]
        s_w, s_c = [], []
        for rb, q, j0 in zip(rbs, qs, j0s):
            kw = jnp.concatenate([kt_scr[j0 + t] for t in range(NA_KROWS // 2)], axis=1)
            edge = jnp.where(rb == 0, 0, jnp.where(rb == n_rb - 1, 2, 1))
            s_w.append(_dot(q, kw) + bias_scr[edge])
            s_c.append(_dot(q, kct_scr[...]))
        nq = NA_QROWS * GRID_W
        for rb, j0, sw, sc in zip(rbs, j0s, s_w, s_c):
            m = jnp.maximum(jnp.max(sw, axis=-1, keepdims=True), jnp.max(sc, axis=-1, keepdims=True))
            p_w = jnp.exp2(sw - m)
            p_c = jnp.exp2(sc - m)
            denom = jnp.sum(p_w, axis=-1, keepdims=True) + jnp.sum(p_c, axis=-1, keepdims=True)
            vw = jnp.concatenate([v_scr[j0 + t] for t in range(NA_KROWS // 2)], axis=0)
            o = _dot(p_w.astype(BF16), vw) + _dot(p_c.astype(BF16), vc)
            o_ref[pl.ds(pl.multiple_of(rb * nq, nq), nq), :] = (o / denom).astype(o_ref.dtype)
        return carry

    lax.fori_loop(0, n_rb // NA_STEP_BLOCKS, row_blocks, 0)


def _na_col_bias(rpb):
    col = np.arange(GRID_W)
    col_start = np.clip(col - NA_KW // 2, 0, GRID_W - NA_KW)
    col_in = (col[None, :] >= col_start[:, None]) & (col[None, :] < col_start[:, None] + NA_KW)
    dc = np.clip(col[None, :] - col[:, None], 1 - NA_KW, NA_KW - 1) + (NA_KW - 1)
    onehot = (dc[None] == np.arange(2 * NA_KW - 1)[:, None, None]).astype(np.float32)
    vals = jnp.einsum("hrk,kqc->hrqc", rpb.astype(F32), onehot, precision=lax.Precision.HIGHEST)
    return jnp.where(col_in[None, None], vals, NEG_BIG)


def _rope_tables(rows):
    quarter = HEAD_DIM // 4
    inv = ROPE_BASE ** (-jnp.arange(quarter, dtype=F32) / quarter)
    ang_r = jnp.arange(rows).astype(F32)[:, None] * inv[None, :]
    ang_c = jnp.arange(GRID_W).astype(F32)[:, None] * inv[None, :]
    half = lambda a: jnp.stack([jnp.concatenate([jnp.cos(a), jnp.cos(a)], axis=-1),
                                jnp.concatenate([-jnp.sin(a), jnp.sin(a)], axis=-1)])
    r, c = half(ang_r), half(ang_c)
    return jnp.concatenate([r, jnp.zeros_like(r)], axis=-1), jnp.concatenate([jnp.zeros_like(c), c], axis=-1)


def _na_call(p_ctx, p_lat, rpb, *, col0, n_heads):
    t_ctx, t = p_ctx.shape[0], p_lat.shape[0]
    rows = t // GRID_W
    assert t % GRID_W == 0 and rows % (NA_QROWS * NA_STEP_BLOCKS) == 0 and rows >= NA_KROWS + NA_QROWS
    assert rows % (2 * NA_PREP_TILES) == 0
    assert NA_QROWS % 2 == 0 and NA_KH % 4 == 0
    nq = NA_QROWS * GRID_W
    rtab, ctab = _rope_tables(rows)
    cbias = _na_col_bias(rpb)
    full = lambda col: pl.BlockSpec((t, HEAD_DIM), lambda h: (0, col + h))
    ctx = lambda col: pl.BlockSpec((t_ctx, HEAD_DIM), lambda h: (0, col + h))
    return pl.pallas_call(
        functools.partial(_na_kernel, rows=rows),
        grid=(n_heads,),
        in_specs=[full(col0), full(col0 + n_heads), full(col0 + 2 * n_heads),
                  ctx(col0 + n_heads), ctx(col0 + 2 * n_heads),
                  pl.BlockSpec((2, rows, HEAD_DIM), lambda h: (0, 0, 0)),
                  pl.BlockSpec((2, GRID_W, HEAD_DIM), lambda h: (0, 0, 0)),
                  pl.BlockSpec((None, 2 * NA_KH - 1, GRID_W, GRID_W), lambda h: (h, 0, 0, 0))],
        out_specs=pl.BlockSpec((t, HEAD_DIM), lambda h: (0, h)),
        out_shape=jax.ShapeDtypeStruct((t, n_heads * HEAD_DIM), BF16),
        scratch_shapes=[pltpu.VMEM((rows // 2, HEAD_DIM, 2 * GRID_W), BF16),
                        pltpu.VMEM((rows // 2, 2 * GRID_W, HEAD_DIM), BF16),
                        pltpu.VMEM((HEAD_DIM, t_ctx), BF16),
                        pltpu.VMEM((3, nq, NA_KROWS * GRID_W), F32)],
        compiler_params=_params("parallel"),
        name="natten",
    )(p_lat, p_lat, p_lat, p_ctx, p_ctx, rtab, ctab, cbias)


def _outproj_kernel(hg_ref, na_ref, h_ref, gate_ref, w_ref, o_ref):
    hw = hg_ref.shape[1]
    mix = _dot(hg_ref[...], w_ref[:hw, :]) + _dot(na_ref[...], w_ref[hw:, :])
    o_ref[...] = h_ref[...] + gate_ref[...] * mix


def _outproj_call(hg, na, h, gate, w, tm=512):
    t, d = h.shape
    hw, nw = hg.shape[1], na.shape[1]
    assert t % tm == 0
    return pl.pallas_call(
        _outproj_kernel,
        grid=(t // tm,),
        in_specs=[pl.BlockSpec((tm, hw), lambda i: (i, 0)),
                  pl.BlockSpec((tm, nw), lambda i: (i, 0)),
                  pl.BlockSpec((tm, d), lambda i: (i, 0)),
                  pl.BlockSpec((1, d), lambda i: (0, 0)),
                  pl.BlockSpec((hw + nw, d), lambda i: (0, 0))],
        out_specs=pl.BlockSpec((tm, d), lambda i: (i, 0)),
        out_shape=jax.ShapeDtypeStruct((t, d), F32),
        compiler_params=_params("parallel"),
        name="outproj",
    )(hg, na, h, gate, w)


def kernel(x, c, ctx, c_ctx, w_mod, b_mod, norm_w, ffn1_w_gate, ffn1_w_up, ffn1_w_down, w_in, hg_lb_logits,
           hg_norm_w, na_rpb, w_out, ffn2_w_gate, ffn2_w_up, ffn2_w_down, final_norm_w):
    batch, _, d = x.shape
    depth = w_mod.shape[0]
    assert batch == 1 and depth == 1
    hg_heads = hg_lb_logits.shape[-1] // HEAD_DIM
    na_heads = na_rpb.shape[1]
    row = lambda v: v.reshape(1, -1)

    cc = jnp.stack([c[0], c_ctx], axis=1)
    mods = _mod_call(cc, w_mod[0], row(b_mod[0]))
    m_l = mods[0].reshape(N_MOD, d)
    m_c = mods[1].reshape(N_MOD, d)
    lb = jnp.cumsum(jax.nn.softmax(hg_lb_logits.astype(F32), axis=1), axis=1)[:, 0]

    h_c, *ffn1 = _ffn_call(ctx[0], m_c[0:3], row(norm_w[0, 0]), ffn1_w_gate[0], ffn1_w_up[0], ffn1_w_down[0])
    h_l, w_in_b = _ffn_call(x[0], m_l[0:3], row(norm_w[0, 0]), *ffn1, side_casts=(w_in[0],))
    p_l = _inproj_call(h_l, m_l[3:5], row(norm_w[0, 1]), w_in_b)
    p_c = _inproj_call(h_c, m_c[3:5], row(norm_w[0, 1]), w_in_b)

    st0 = _hg_ctx_call(p_c, lb, n_heads=hg_heads)
    o_fw, wg2, w_out_b = _hg_call(p_l, st0[0], lb[0:1], None, None, rev=False, n_heads=hg_heads,
                                  side_casts=(ffn2_w_gate[0], w_out[0]))
    hg, wu2, wd2 = _hg_call(p_l, st0[1], lb[1:2], o_fw, row(hg_norm_w[0]), rev=True, n_heads=hg_heads,
                            side_casts=(ffn2_w_up[0], ffn2_w_down[0]))
    ffn2 = (wg2, wu2, wd2)
    na = _na_call(p_c, p_l, na_rpb[0], col0=5 * hg_heads, n_heads=na_heads)

    h_l = _outproj_call(hg, na, h_l, m_l[5:6], w_out_b)
    out, = _ffn_call(h_l, m_l[6:9], row(norm_w[0, 2]), *ffn2, final_nw=row(final_norm_w))
    return out[None]
```

```python
import functools

import numpy as np
import jax
import jax.numpy as jnp
from jax import lax
from jax.experimental import pallas as pl
from jax.experimental.pallas import tpu as pltpu

F32 = jnp.float32
BF16 = jnp.bfloat16

EPS = 1e-6
GRID_W = 64
HEAD_DIM = 128
NA_KH = 8
NA_KW = 16
ROPE_BASE = 10000.0
N_MOD = 9
NEG_BIG = -1e30
LOG2E = 1.4426950408889634

V7X_VMEM_BYTES = 64 * 1024 * 1024
VMEM_LIMIT = V7X_VMEM_BYTES - 8 * 1024 * 1024

HG_CHUNK = 128
HG_LEVELS = (64, 32, 16, 8, 4, 2, 1)
NA_QROWS = 4
NA_KROWS = NA_QROWS + NA_KH
NA_STEP_BLOCKS = 4
NA_PREP_TILES = 8


def _dot(a, b):
    return jnp.dot(a, b, preferred_element_type=F32)


def _dot_nt(a, b):
    return lax.dot_general(a, b, (((1,), (1,)), ((), ())), preferred_element_type=F32)


def _dot_tn(a, b):
    return lax.dot_general(a, b, (((0,), (0,)), ((), ())), preferred_element_type=F32)


def _sigmoid(x):
    return 0.5 * jnp.tanh(0.5 * x) + 0.5


def _silu(x):
    return x * _sigmoid(x)


def _rmsnorm(x, w):
    return x * lax.rsqrt(jnp.mean(x * x, axis=-1, keepdims=True) + EPS) * w


def _params(*semantics):
    return pltpu.CompilerParams(dimension_semantics=semantics, vmem_limit_bytes=VMEM_LIMIT)


def _mod_kernel(c_ref, w_ref, b_ref, o_ref, sc_scr):
    d, tn = w_ref.shape
    n_vec = c_ref.shape[1]
    sub, lanes = 8, sc_scr.shape[-1]

    @pl.when(pl.program_id(0) == 0)
    def _():
        sc = _silu(c_ref[...])
        for r in range(n_vec):
            sc_scr[r] = jnp.broadcast_to(sc[:, r:r + 1], (d, lanes))

    def body(i, accs):
        rows = pl.ds(pl.multiple_of(i * sub, sub), sub)
        w = w_ref[rows, :]
        return tuple(acc + w * jnp.concatenate([sc_scr[r, rows, :]] * (tn // lanes), axis=1)
                     for r, acc in enumerate(accs))

    zero = jnp.zeros((sub, tn), F32)
    accs = lax.fori_loop(0, d // sub, body, (zero,) * n_vec, unroll=8)
    for r, acc in enumerate(accs):
        o_ref[r:r + 1, :] = jnp.sum(acc, axis=0, keepdims=True) + b_ref[...]


def _mod_call(cc, w_mod, b_mod):
    d, n_vec = cc.shape
    n = w_mod.shape[1]
    tn = next(t for t in (1024, 512, 256, 128) if n % t == 0)
    return pl.pallas_call(
        _mod_kernel,
        grid=(n // tn,),
        in_specs=[pl.BlockSpec((d, n_vec), lambda j: (0, 0)),
                  pl.BlockSpec((d, tn), lambda j: (0, j)),
                  pl.BlockSpec((1, tn), lambda j: (0, j))],
        out_specs=pl.BlockSpec((n_vec, tn), lambda j: (0, j)),
        out_shape=jax.ShapeDtypeStruct((n_vec, n), F32),
        scratch_shapes=[pltpu.VMEM((n_vec, d, HEAD_DIM), F32)],
        compiler_params=_params("arbitrary"),
        name="mod",
    )(cc, w_mod, b_mod)


def _adaln(x, nw, mod_ref):
    return (_rmsnorm(x, nw) * (1.0 + mod_ref[1:2, :]) + mod_ref[0:1, :]).astype(BF16)


def _ffn_kernel(*refs, nf, final_norm, emit_weights, n_side):
    refs = list(refs)
    x_ref, mod_ref, nw_ref, wg_ref, wu_ref, wd_ref = refs[:6]
    del refs[:6]
    fnw_ref = refs.pop(0) if final_norm else None
    side_in = [refs.pop(0) for _ in range(n_side)]
    o_ref = refs.pop(0)
    w_out = [refs.pop(0) for _ in range(3)] if emit_weights else None
    side_out = [refs.pop(0) for _ in range(n_side)]
    u_scr, acc_scr = refs
    f = pl.program_id(1)

    for src, dst in zip(side_in, side_out):
        dst[...] = src[...].astype(BF16)

    def tile_product(u):
        wg, wu, wd = wg_ref[...], wu_ref[...], wd_ref[...]
        if emit_weights:
            wg, wu, wd = wg.astype(BF16), wu.astype(BF16), wd.astype(BF16)
            for dst, w in zip(w_out, (wg, wu, wd)):
                dst[...] = w
        a = (_silu(_dot(u, wg)) * _dot(u, wu)).astype(BF16)
        return _dot(a, wd)

    @pl.when(f == 0)
    def _():
        u = _adaln(x_ref[...], nw_ref[...], mod_ref)
        u_scr[...] = u
        acc_scr[...] = tile_product(u)

    @pl.when((f > 0) & (f < nf - 1))
    def _():
        acc_scr[...] += tile_product(u_scr[...])

    @pl.when(f == nf - 1)
    def _():
        h = x_ref[...] + 0.5 * mod_ref[2:3, :] * (acc_scr[...] + tile_product(u_scr[...]))
        if final_norm:
            h = _rmsnorm(h, fnw_ref[...])
        o_ref[...] = h


def _side_cast_spec(shape, n_blk, nf):
    rows, cols = shape
    assert rows % (16 * n_blk) == 0
    n_col = next(c for c in range(nf, 0, -1) if cols % c == 0 and (cols // c) % HEAD_DIM == 0)
    return pl.BlockSpec((rows // n_blk, cols // n_col), lambda i, f: (i, jnp.minimum(f, n_col - 1)))


def _ffn_call(x, mod3, nw, wg, wu, wd, final_nw=None, side_casts=(), tm=512, tf=512):
    t, d = x.shape
    dff = wd.shape[0]
    tm = min(tm, t)
    nf, n_blk = dff // tf, t // tm
    assert t % tm == 0 and dff % tf == 0 and nf >= 2
    final_norm = final_nw is not None
    emit_weights = wg.dtype == F32
    assert not emit_weights or n_blk == 1
    flat = pl.BlockSpec((d, tf), lambda i, f: (0, f))
    tiled = pl.BlockSpec((None, d, tf), lambda i, f: (f, 0, 0))
    down = pl.BlockSpec((tf, d), lambda i, f: (f, 0))
    w_specs = [tiled if wg.ndim == 3 else flat, tiled if wu.ndim == 3 else flat, down]
    side_specs = [_side_cast_spec(a.shape, n_blk, nf) for a in side_casts]
    in_specs = [pl.BlockSpec((tm, d), lambda i, f: (i, 0)),
                pl.BlockSpec((3, d), lambda i, f: (0, 0)),
                pl.BlockSpec((1, d), lambda i, f: (0, 0))] + w_specs
    args = [x, mod3, nw, wg, wu, wd]
    if final_norm:
        in_specs.append(pl.BlockSpec((1, d), lambda i, f: (0, 0)))
        args.append(final_nw)
    in_specs += side_specs
    args += list(side_casts)
    out_specs = [pl.BlockSpec((tm, d), lambda i, f: (i, 0))]
    out_shape = [jax.ShapeDtypeStruct((t, d), F32)]
    if emit_weights:
        out_specs += [tiled, tiled, down]
        out_shape += [jax.ShapeDtypeStruct((nf, d, tf), BF16)] * 2 + [jax.ShapeDtypeStruct(wd.shape, BF16)]
    out_specs += side_specs
    out_shape += [jax.ShapeDtypeStruct(a.shape, BF16) for a in side_casts]
    return pl.pallas_call(
        functools.partial(_ffn_kernel, nf=nf, final_norm=final_norm, emit_weights=emit_weights,
                          n_side=len(side_casts)),
        grid=(n_blk, nf),
        in_specs=in_specs,
        out_specs=out_specs,
        out_shape=out_shape,
        scratch_shapes=[pltpu.VMEM((tm, d), BF16), pltpu.VMEM((tm, d), F32)],
        compiler_params=_params("parallel", "arbitrary"),
        name="ffn",
    )(*args)


def _inproj_kernel(x_ref, mod_ref, nw_ref, w_ref, o_ref, u_scr):
    @pl.when(pl.program_id(1) == 0)
    def _():
        u = _adaln(x_ref[...], nw_ref[...], mod_ref)
        u_scr[...] = u
        o_ref[...] = _dot(u, w_ref[...])

    @pl.when(pl.program_id(1) > 0)
    def _():
        o_ref[...] = _dot(u_scr[...], w_ref[...])


def _inproj_call(x, mod2, nw, w, tm=1024, tn=1024):
    t, d = x.shape
    n = w.shape[1]
    tm = min(tm, t)
    assert t % tm == 0 and n % tn == 0
    return pl.pallas_call(
        _inproj_kernel,
        grid=(t // tm, n // tn),
        in_specs=[pl.BlockSpec((tm, d), lambda i, j: (i, 0)),
                  pl.BlockSpec((2, d), lambda i, j: (0, 0)),
                  pl.BlockSpec((1, d), lambda i, j: (0, 0)),
                  pl.BlockSpec((d, tn), lambda i, j: (0, j))],
        out_specs=pl.BlockSpec((tm, tn), lambda i, j: (i, j)),
        out_shape=jax.ShapeDtypeStruct((t, n), F32),
        scratch_shapes=[pltpu.VMEM((tm, d), BF16)],
        compiler_params=_params("parallel", "arbitrary"),
        name="inproj",
    )(x, mod2, nw, w)


def _split3(x):
    hi = x.astype(BF16)
    r = x - hi.astype(F32)
    mid = r.astype(BF16)
    lo = (r - mid.astype(F32)).astype(BF16)
    return hi, mid, lo


def _hg_gates(q_raw, f_raw, lb, tri):
    q = _silu(q_raw)
    f = lb + (1.0 - lb) * _sigmoid(f_raw)
    hi, mid, lo = _split3(jnp.log(f) * LOG2E)
    return q, 1.0 - f, f, _dot(tri, hi) + _dot(tri, mid) + _dot(tri, lo)


def _hg_level_operand(m, q, k, f, b, b_row, pos, rev):
    c = HG_CHUNK
    if m >= 8:
        qk, bmid = [], []
        for a in range(c // (2 * m)):
            lo_r, mid_r, hi_r = a * 2 * m, a * 2 * m + m, (a + 1) * 2 * m
            qk += [q[lo_r:mid_r], k[mid_r:hi_r]] if rev else [k[lo_r:mid_r], q[mid_r:hi_r]]
            bmid.append(jnp.broadcast_to(b_row(mid_r if rev else mid_r - 1), (2 * m, HEAD_DIM)))
        qk = jnp.concatenate(qk, axis=0)
        bmid = bmid[0] if len(bmid) == 1 else jnp.concatenate(bmid, axis=0)
        return (qk * jnp.exp2(-jnp.abs(b - bmid))).astype(BF16)
    half = pos & (2 * m - 1)
    q_side = (half < m) if rev else (half >= m)
    if m == 4:
        rows = [jnp.broadcast_to(b_row(a * 8 + (4 if rev else 3)), (8, HEAD_DIM)) for a in range(c // 8)]
        w = jnp.exp2(-jnp.abs(b - jnp.concatenate(rows, axis=0)))
    elif m == 2:
        f_next = pltpu.roll(f, c - 1, 0)
        f_prev = pltpu.roll(f, 1, 0)
        if rev:
            w = jnp.where(half == 0, f * f_next, jnp.where(half == 1, f, jnp.where(half == 2, 1.0, f_prev)))
        else:
            w = jnp.where(half == 0, f_next, jnp.where(half == 1, 1.0, jnp.where(half == 2, f, f * f_prev)))
    else:
        w = jnp.where(q_side, f, 1.0)
    return (jnp.where(q_side, q, k) * w).astype(BF16)


def _hg_state_step(st, k, vb, b, b_end):
    return st * jnp.exp2(b_end) + _dot_tn(vb, (k * jnp.exp2(b_end - b)).astype(BF16))


def _hg_ctx_kernel(q_ref, ff_ref, fb_ref, v_ref, lb_ref, tri_ref, o_ref, b_scr):
    c = HG_CHUNK
    n_chunks = q_ref.shape[0] // c
    for d, (f_ref, rev) in enumerate(((ff_ref, False), (fb_ref, True))):
        st = jnp.zeros((HEAD_DIM, HEAD_DIM), F32)
        for ci in (range(n_chunks - 1, -1, -1) if rev else range(n_chunks)):
            sl = pl.ds(ci * c, c)
            _, k, _, b = _hg_gates(q_ref[sl, :], f_ref[sl, :], lb_ref[d:d + 1, :], tri_ref[d])
            b_scr[...] = b
            st = _hg_state_step(st, k, v_ref[sl, :].astype(BF16), b, b_scr[pl.ds(0 if rev else c - 1, 1), :])
        o_ref[d] = st


def _hg_kernel(q_ref, f_ref, v_ref, st0_ref, lb_ref, tri_ref, lvl_ref, *rest, rev, heads_per_step, unroll,
               readout, n_side):
    rest = list(rest)
    of_ref, g_ref, nw_ref = (rest.pop(0), rest.pop(0), rest.pop(0)) if readout else (None, None, None)
    side_in = [rest.pop(0) for _ in range(n_side)]
    o_ref = rest.pop(0)
    side_out = [rest.pop(0) for _ in range(n_side)]
    b_scr, st_scr = rest
    c = HG_CHUNK

    for src, dst in zip(side_in, side_out):
        dst[...] = src[...].astype(BF16)

    @pl.when(pl.program_id(1) == 0)
    def _():
        st_scr[...] = st0_ref[...]

    n_chunks = q_ref.shape[0] // c
    tri = tri_ref[...]

    def body(i, carry):
        where = []
        for u in range(unroll):
            ci = i * unroll + u
            ci = (n_chunks - 1 - ci) if rev else ci
            rows = pl.ds(pl.multiple_of(ci * c, c), c)
            where += [(rows, slice(h * HEAD_DIM, (h + 1) * HEAD_DIM)) for h in range(heads_per_step)]
        n = len(where)
        gates = [_hg_gates(q_ref[r, cs], f_ref[r, cs], lb_ref[:, cs], tri) for r, cs in where]
        for p in range(n):
            b_scr[p] = gates[p][3]
        vbs = [v_ref[r, cs].astype(BF16) for r, cs in where]

        pos = lax.broadcasted_iota(jnp.int32, (c, HEAD_DIM), 0)
        lvl = lvl_ref[...]
        scores = [0.0] * n
        for m in HG_LEVELS:
            for p in range(n):
                q, k, f, b = gates[p]
                x = _hg_level_operand(m, q, k, f, b, lambda r, p=p: b_scr[p, pl.ds(r, 1), :], pos, rev)
                scores[p] = jnp.where(lvl == m, _dot_nt(x, x), scores[p])

        outs = [None] * n
        for h in range(heads_per_step):
            st = st_scr[h]
            for u in range(unroll):
                p = u * heads_per_step + h
                q, k, f, b = gates[p]
                outs[p] = _dot_nt((q * jnp.exp2(b)).astype(BF16), st.astype(BF16))
                st = _hg_state_step(st, k, vbs[p], b, b_scr[p, pl.ds(0 if rev else c - 1, 1), :])
            st_scr[h] = st

        for p, (r, cs) in enumerate(where):
            q, k, f, b = gates[p]
            o = outs[p] + _dot(scores[p].astype(BF16), vbs[p])
            o = o + jnp.sum(q * k, axis=-1, keepdims=True) * v_ref[r, cs]
            if readout:
                o = _rmsnorm(of_ref[r, cs] + o, nw_ref[...]) * _silu(g_ref[r, cs])
            o_ref[r, cs] = o.astype(o_ref.dtype)
        return carry

    lax.fori_loop(0, n_chunks // unroll, body, 0)


def _hg_tables(rev):
    t = np.arange(HG_CHUNK)[:, None]
    s = np.arange(HG_CHUNK)[None, :]
    x = t ^ s
    msb = np.where(x > 0, 2 ** np.floor(np.log2(np.maximum(x, 1))).astype(np.int64), 0)
    lvl = np.where((s > t) if rev else (s < t), msb, 0).astype(np.int32)
    tri = np.tril(np.ones((HG_CHUNK, HG_CHUNK), np.float32))
    return (tri.T if rev else tri), lvl


def _hg_ctx_call(p_ctx, lb, *, n_heads):
    t_ctx = p_ctx.shape[0]
    assert t_ctx % HG_CHUNK == 0
    tri = jnp.asarray(np.stack([_hg_tables(False)[0], _hg_tables(True)[0]]), BF16)
    col = lambda c0: pl.BlockSpec((t_ctx, HEAD_DIM), lambda h: (0, c0 + h))
    const = pl.BlockSpec((2, HG_CHUNK, HG_CHUNK), lambda h: (0, 0, 0))
    return pl.pallas_call(
        _hg_ctx_kernel,
        grid=(n_heads,),
        in_specs=[col(0), col(n_heads), col(2 * n_heads), col(3 * n_heads),
                  pl.BlockSpec((2, HEAD_DIM), lambda h: (0, h)), const],
        out_specs=pl.BlockSpec((2, None, HEAD_DIM, HEAD_DIM), lambda h: (0, h, 0, 0)),
        out_shape=jax.ShapeDtypeStruct((2, n_heads, HEAD_DIM, HEAD_DIM), F32),
        scratch_shapes=[pltpu.VMEM((HG_CHUNK, HEAD_DIM), F32)],
        compiler_params=_params("parallel"),
        name="hgrn_ctx",
    )(p_ctx, p_ctx, p_ctx, p_ctx, lb, tri)


def _hg_call(p_lat, st0, lb_dir, o_fw, norm_w, *, rev, n_heads, side_casts=(), tb=1024, heads_per_step=2,
             unroll=4):
    t_lat = p_lat.shape[0]
    tb = min(tb, t_lat)
    hb = heads_per_step
    assert t_lat % tb == 0 and tb % (HG_CHUNK * unroll) == 0 and n_heads % hb == 0
    n_blk, n_grp = t_lat // tb, n_heads // hb
    readout = o_fw is not None
    blk_row = (lambda j: n_blk - 1 - j) if rev else (lambda j: j)
    lat = lambda c0: pl.BlockSpec((tb, hb * HEAD_DIM), lambda g, j: (blk_row(j), c0 * n_grp + g))
    tri, lvl = _hg_tables(rev)
    const = pl.BlockSpec((HG_CHUNK, HG_CHUNK), lambda g, j: (0, 0))
    in_specs = [lat(0), lat(2 if rev else 1), lat(3),
                pl.BlockSpec((hb, HEAD_DIM, HEAD_DIM), lambda g, j: (g, 0, 0)),
                pl.BlockSpec((1, hb * HEAD_DIM), lambda g, j: (0, g)), const, const]
    args = [p_lat, p_lat, p_lat, st0, lb_dir, jnp.asarray(tri, BF16), jnp.asarray(lvl)]
    out_spec = pl.BlockSpec((tb, hb * HEAD_DIM), lambda g, j: (blk_row(j), g))
    if readout:
        in_specs += [out_spec, lat(4), pl.BlockSpec((1, HEAD_DIM), lambda g, j: (0, 0))]
        args += [o_fw, p_lat, norm_w]
    side_specs = [_side_cast_spec(a.shape, n_grp, n_blk) for a in side_casts]
    return pl.pallas_call(
        functools.partial(_hg_kernel, rev=rev, heads_per_step=hb, unroll=unroll, readout=readout,
                          n_side=len(side_casts)),
        grid=(n_grp, n_blk),
        in_specs=in_specs + side_specs,
        out_specs=[out_spec] + side_specs,
        out_shape=[jax.ShapeDtypeStruct((t_lat, n_heads * HEAD_DIM), BF16 if readout else F32)]
        + [jax.ShapeDtypeStruct(a.shape, BF16) for a in side_casts],
        scratch_shapes=[pltpu.VMEM((unroll * hb, HG_CHUNK, HEAD_DIM), F32),
                        pltpu.VMEM((hb, HEAD_DIM, HEAD_DIM), F32)],
        compiler_params=_params("parallel", "arbitrary"),
        name="hgrn_bwd" if rev else "hgrn_fwd",
    )(*args, *side_casts)


def _rope(x, cos, sin_signed, low):
    return x * cos + jnp.where(low, pltpu.roll(x, 96, 1), pltpu.roll(x, 32, 1)) * sin_signed


def _na_kernel(q_ref, k_ref, v_ref, kc_ref, vc_ref, rtab_ref, ctab_ref, cbias_ref,
               o_ref, kt_scr, v_scr, kct_scr, bias_scr, *, rows):
    tile = 2 * GRID_W
    n_rb = rows // NA_QROWS
    lane = lax.broadcasted_iota(jnp.int32, (GRID_W, HEAD_DIM), 1)
    low = (lane & (HEAD_DIM // 2 - 1)) < HEAD_DIM // 4

    def rope_rows(x_ref, row0, n):
        out = []
        for i in range(n):
            x = x_ref[pl.ds(pl.multiple_of((row0 + i) * GRID_W, GRID_W), GRID_W), :]
            cos = rtab_ref[0, pl.ds(row0 + i, 1), :] + ctab_ref[0]
            sin = rtab_ref[1, pl.ds(row0 + i, 1), :] + ctab_ref[1]
            out.append(_rope(x, cos, sin, low))
        return jnp.concatenate(out, axis=0)

    def prep(i, carry):
        js = [i * NA_PREP_TILES + u for u in range(NA_PREP_TILES)]
        roped = [rope_rows(k_ref, 2 * j, 2) for j in js]
        for j, kr in zip(js, roped):
            kt_scr[j] = kr.T.astype(BF16)
            v_scr[j] = v_ref[pl.ds(pl.multiple_of(j * tile, tile), tile), :].astype(BF16)
        return carry

    lax.fori_loop(0, rows // (2 * NA_PREP_TILES), prep, 0)
    kct_scr[...] = kc_ref[...].T.astype(BF16)

    bias_scr[...] = jnp.full(bias_scr.shape, NEG_BIG, F32)
    for e in range(3):
        for a in range(NA_QROWS):
            b0 = (0, a, NA_KROWS - NA_KH)[e]
            d0 = (NA_KH - 1 - a, NA_KH // 2 - 1, NA_QROWS - 1 - a)[e]
            for i in range(NA_KH):
                bias_scr[e, a * GRID_W:(a + 1) * GRID_W, (b0 + i) * GRID_W:(b0 + i + 1) * GRID_W] = (
                    cbias_ref[d0 + i] * LOG2E)

    vc = vc_ref[...].astype(BF16)

    def row_blocks(i, carry):
        rbs = [i * NA_STEP_BLOCKS + u for u in range(NA_STEP_BLOCKS)]
        qs = [(rope_rows(q_ref, rb * NA_QROWS, NA_QROWS) * (HEAD_DIM ** -0.5 * LOG2E)).astype(BF16) for rb in rbs]
        j0s = [jnp.clip(rb * (NA_QROWS // 2) - NA_KH // 4, 0, (rows - NA_KROWS) // 2) for rb in rbs]
        s_w, s_c = [], []
        for rb, q, j0 in zip(rbs, qs, j0s):
            kw = jnp.concatenate([kt_scr[j0 + t] for t in range(NA_KROWS // 2)], axis=1)
            edge = jnp.where(rb == 0, 0, jnp.where(rb == n_rb - 1, 2, 1))
            s_w.append(_dot(q, kw) + bias_scr[edge])
            s_c.append(_dot(q, kct_scr[...]))
        nq = NA_QROWS * GRID_W
        for rb, j0, sw, sc in zip(rbs, j0s, s_w, s_c):
            m = jnp.maximum(jnp.max(sw, axis=-1, keepdims=True), jnp.max(sc, axis=-1, keepdims=True))
            p_w = jnp.exp2(sw - m)
            p_c = jnp.exp2(sc - m)
            denom = jnp.sum(p_w, axis=-1, keepdims=True) + jnp.sum(p_c, axis=-1, keepdims=True)
            vw = jnp.concatenate([v_scr[j0 + t] for t in range(NA_KROWS // 2)], axis=0)
            o = _dot(p_w.astype(BF16), vw) + _dot(p_c.astype(BF16), vc)
            o_ref[pl.ds(pl.multiple_of(rb * nq, nq), nq), :] = (o / denom).astype(o_ref.dtype)
        return carry

    lax.fori_loop(0, n_rb // NA_STEP_BLOCKS, row_blocks, 0)


def _na_col_bias(rpb):
    col = np.arange(GRID_W)
    col_start = np.clip(col - NA_KW // 2, 0, GRID_W - NA_KW)
    col_in = (col[None, :] >= col_start[:, None]) & (col[None, :] < col_start[:, None] + NA_KW)
    dc = np.clip(col[None, :] - col[:, None], 1 - NA_KW, NA_KW - 1) + (NA_KW - 1)
    onehot = (dc[None] == np.arange(2 * NA_KW - 1)[:, None, None]).astype(np.float32)
    vals = jnp.einsum("hrk,kqc->hrqc", rpb.astype(F32), onehot, precision=lax.Precision.HIGHEST)
    return jnp.where(col_in[None, None], vals, NEG_BIG)


def _rope_tables(rows):
    quarter = HEAD_DIM // 4
    inv = ROPE_BASE ** (-jnp.arange(quarter, dtype=F32) / quarter)
    ang_r = jnp.arange(rows).astype(F32)[:, None] * inv[None, :]
    ang_c = jnp.arange(GRID_W).astype(F32)[:, None] * inv[None, :]
    half = lambda a: jnp.stack([jnp.concatenate([jnp.cos(a), jnp.cos(a)], axis=-1),
                                jnp.concatenate([-jnp.sin(a), jnp.sin(a)], axis=-1)])
    r, c = half(ang_r), half(ang_c)
    return jnp.concatenate([r, jnp.zeros_like(r)], axis=-1), jnp.concatenate([jnp.zeros_like(c), c], axis=-1)


def _na_call(p_ctx, p_lat, rpb, *, col0, n_heads):
    t_ctx, t = p_ctx.shape[0], p_lat.shape[0]
    rows = t // GRID_W
    assert t % GRID_W == 0 and rows % (NA_QROWS * NA_STEP_BLOCKS) == 0 and rows >= NA_KROWS + NA_QROWS
    assert rows % (2 * NA_PREP_TILES) == 0
    assert NA_QROWS % 2 == 0 and NA_KH % 4 == 0
    nq = NA_QROWS * GRID_W
    rtab, ctab = _rope_tables(rows)
    cbias = _na_col_bias(rpb)
    full = lambda col: pl.BlockSpec((t, HEAD_DIM), lambda h: (0, col + h))
    ctx = lambda col: pl.BlockSpec((t_ctx, HEAD_DIM), lambda h: (0, col + h))
    return pl.pallas_call(
        functools.partial(_na_kernel, rows=rows),
        grid=(n_heads,),
        in_specs=[full(col0), full(col0 + n_heads), full(col0 + 2 * n_heads),
                  ctx(col0 + n_heads), ctx(col0 + 2 * n_heads),
                  pl.BlockSpec((2, rows, HEAD_DIM), lambda h: (0, 0, 0)),
                  pl.BlockSpec((2, GRID_W, HEAD_DIM), lambda h: (0, 0, 0)),
                  pl.BlockSpec((None, 2 * NA_KH - 1, GRID_W, GRID_W), lambda h: (h, 0, 0, 0))],
        out_specs=pl.BlockSpec((t, HEAD_DIM), lambda h: (0, h)),
        out_shape=jax.ShapeDtypeStruct((t, n_heads * HEAD_DIM), BF16),
        scratch_shapes=[pltpu.VMEM((rows // 2, HEAD_DIM, 2 * GRID_W), BF16),
                        pltpu.VMEM((rows // 2, 2 * GRID_W, HEAD_DIM), BF16),
                        pltpu.VMEM((HEAD_DIM, t_ctx), BF16),
                        pltpu.VMEM((3, nq, NA_KROWS * GRID_W), F32)],
        compiler_params=_params("parallel"),
        name="natten",
    )(p_lat, p_lat, p_lat, p_ctx, p_ctx, rtab, ctab, cbias)


def _outproj_kernel(hg_ref, na_ref, h_ref, gate_ref, w_ref, o_ref):
    hw = hg_ref.shape[1]
    mix = _dot(hg_ref[...], w_ref[:hw, :]) + _dot(na_ref[...], w_ref[hw:, :])
    o_ref[...] = h_ref[...] + gate_ref[...] * mix


def _outproj_call(hg, na, h, gate, w, tm=512):
    t, d = h.shape
    hw, nw = hg.shape[1], na.shape[1]
    assert t % tm == 0
    return pl.pallas_call(
        _outproj_kernel,
        grid=(t // tm,),
        in_specs=[pl.BlockSpec((tm, hw), lambda i: (i, 0)),
                  pl.BlockSpec((tm, nw), lambda i: (i, 0)),
                  pl.BlockSpec((tm, d), lambda i: (i, 0)),
                  pl.BlockSpec((1, d), lambda i: (0, 0)),
                  pl.BlockSpec((hw + nw, d), lambda i: (0, 0))],
        out_specs=pl.BlockSpec((tm, d), lambda i: (i, 0)),
        out_shape=jax.ShapeDtypeStruct((t, d), F32),
        compiler_params=_params("parallel"),
        name="outproj",
    )(hg, na, h, gate, w)


def kernel(x, c, ctx, c_ctx, w_mod, b_mod, norm_w, ffn1_w_gate, ffn1_w_up, ffn1_w_down, w_in, hg_lb_logits,
           hg_norm_w, na_rpb, w_out, ffn2_w_gate, ffn2_w_up, ffn2_w_down, final_norm_w):
    batch, _, d = x.shape
    depth = w_mod.shape[0]
    assert batch == 1 and depth == 1
    hg_heads = hg_lb_logits.shape[-1] // HEAD_DIM
    na_heads = na_rpb.shape[1]
    row = lambda v: v.reshape(1, -1)

    cc = jnp.stack([c[0], c_ctx], axis=1)
    mods = _mod_call(cc, w_mod[0], row(b_mod[0]))
    m_l = mods[0].reshape(N_MOD, d)
    m_c = mods[1].reshape(N_MOD, d)
    lb = jnp.cumsum(jax.nn.softmax(hg_lb_logits.astype(F32), axis=1), axis=1)[:, 0]

    h_c, *ffn1 = _ffn_call(ctx[0], m_c[0:3], row(norm_w[0, 0]), ffn1_w_gate[0], ffn1_w_up[0], ffn1_w_down[0])
    h_l, w_in_b = _ffn_call(x[0], m_l[0:3], row(norm_w[0, 0]), *ffn1, side_casts=(w_in[0],))
    p_l = _inproj_call(h_l, m_l[3:5], row(norm_w[0, 1]), w_in_b)
    p_c = _inproj_call(h_c, m_c[3:5], row(norm_w[0, 1]), w_in_b)

    st0 = _hg_ctx_call(p_c, lb, n_heads=hg_heads)
    o_fw, wg2, w_out_b = _hg_call(p_l, st0[0], lb[0:1], None, None, rev=False, n_heads=hg_heads,
                                  side_casts=(ffn2_w_gate[0], w_out[0]))
    hg, wu2, wd2 = _hg_call(p_l, st0[1], lb[1:2], o_fw, row(hg_norm_w[0]), rev=True, n_heads=hg_heads,
                            side_casts=(ffn2_w_up[0], ffn2_w_down[0]))
    ffn2 = (wg2, wu2, wd2)
    na = _na_call(p_c, p_l, na_rpb[0], col0=5 * hg_heads, n_heads=na_heads)

    h_l = _outproj_call(hg, na, h_l, m_l[5:6], w_out_b)
    out, = _ffn_call(h_l, m_l[6:9], row(norm_w[0, 2]), *ffn2, final_nw=row(final_norm_w))
    return out[None]
```

```python
import functools

import numpy as np
import jax
import jax.numpy as jnp
from jax import lax
from jax.experimental import pallas as pl
from jax.experimental.pallas import tpu as pltpu

F32 = jnp.float32
BF16 = jnp.bfloat16

EPS = 1e-6
GRID_W = 64
HEAD_DIM = 128
NA_KH = 8
NA_KW = 16
ROPE_BASE = 10000.0
N_MOD = 9
NEG_BIG = -1e30
LOG2E = 1.4426950408889634

LANES = 128
SUBLANES = 8
BF16_SUBLANES = 16
V7X_VMEM_BYTES = 64 * 1024 * 1024
VMEM_LIMIT = V7X_VMEM_BYTES - 8 * 1024 * 1024

HG_CHUNK = 128
HG_LEVELS = (64, 32, 16, 8, 4, 2, 1)
FFN_TF = 512
FFN_HEAD_TF = 256
NA_QROWS = 4
NA_KROWS = NA_QROWS + NA_KH
NA_STEP_BLOCKS = 4
NA_PREP_TILES = 8


def _dot(a, b):
    return jnp.dot(a, b, preferred_element_type=F32)


def _dot_nt(a, b):
    return lax.dot_general(a, b, (((1,), (1,)), ((), ())), preferred_element_type=F32)


def _dot_tn(a, b):
    return lax.dot_general(a, b, (((0,), (0,)), ((), ())), preferred_element_type=F32)


def _sigmoid(x):
    return 0.5 * jnp.tanh(0.5 * x) + 0.5


def _silu(x):
    return x * _sigmoid(x)


def _rmsnorm(x, w):
    return x * lax.rsqrt(jnp.mean(x * x, axis=-1, keepdims=True) + EPS) * w


def _params(*semantics):
    return pltpu.CompilerParams(dimension_semantics=semantics, vmem_limit_bytes=VMEM_LIMIT)


def _mod_kernel(c_ref, w_ref, b_ref, o_ref, sc_scr):
    d, tn = w_ref.shape
    n_vec = c_ref.shape[1]
    sub, lanes = SUBLANES, sc_scr.shape[-1]

    @pl.when(pl.program_id(0) == 0)
    def _():
        sc = _silu(c_ref[...])
        for r in range(n_vec):
            sc_scr[r] = jnp.broadcast_to(sc[:, r:r + 1], (d, lanes))

    def body(i, accs):
        rows = pl.ds(pl.multiple_of(i * sub, sub), sub)
        w = w_ref[rows, :]
        return tuple(acc + w * jnp.concatenate([sc_scr[r, rows, :]] * (tn // lanes), axis=1)
                     for r, acc in enumerate(accs))

    zero = jnp.zeros((sub, tn), F32)
    accs = lax.fori_loop(0, d // sub, body, (zero,) * n_vec, unroll=8)
    for r, acc in enumerate(accs):
        o_ref[r:r + 1, :] = jnp.sum(acc, axis=0, keepdims=True) + b_ref[...]


def _mod_call(cc, w_mod, b_mod):
    d, n_vec = cc.shape
    n = w_mod.shape[1]
    tn = next(t for t in (1024, 512, 256, 128) if n % t == 0)
    return pl.pallas_call(
        _mod_kernel,
        grid=(n // tn,),
        in_specs=[pl.BlockSpec((d, n_vec), lambda j: (0, 0)),
                  pl.BlockSpec((d, tn), lambda j: (0, j)),
                  pl.BlockSpec((1, tn), lambda j: (0, j))],
        out_specs=pl.BlockSpec((n_vec, tn), lambda j: (0, j)),
        out_shape=jax.ShapeDtypeStruct((n_vec, n), F32),
        scratch_shapes=[pltpu.VMEM((n_vec, d, LANES), F32)],
        compiler_params=_params("arbitrary"),
        name="mod",
    )(cc, w_mod, b_mod)


def _adaln(x, nw, mod_ref):
    return (_rmsnorm(x, nw) * (1.0 + mod_ref[1:2, :]) + mod_ref[0:1, :]).astype(BF16)


def _ffn_kernel(*refs, nf, final_norm, emit_weights, n_side, with_ctx, with_head):
    refs = list(refs)
    x_ref, mod_ref, nw_ref, wg_ref, wu_ref, wd_ref = refs[:6]
    del refs[:6]
    fnw_ref = refs.pop(0) if final_norm else None
    xc_ref, modc_ref = (refs.pop(0), refs.pop(0)) if with_ctx else (None, None)
    head_ref = refs.pop(0) if with_head else None
    side_in = [refs.pop(0) for _ in range(n_side)]
    o_ref = refs.pop(0)
    oc_ref = refs.pop(0) if with_ctx else None
    w_out = [refs.pop(0) for _ in range(3)] if emit_weights else None
    side_out = [refs.pop(0) for _ in range(n_side)]
    u_scr, acc_scr = refs
    i, f = pl.program_id(0), pl.program_id(1)
    n_ctx = xc_ref.shape[0] if with_ctx else 0
    live = (i > 0) if with_head else (i >= 0)

    for src, dst in zip(side_in, side_out):
        dst[...] = src[...].astype(BF16)

    def tile_product(u):
        wg, wu, wd = wg_ref[...], wu_ref[...], wd_ref[...]
        if emit_weights:
            wg, wu, wd = wg.astype(BF16), wu.astype(BF16), wd.astype(BF16)
            for dst, w in zip(w_out, (wg, wu, wd)):
                dst[...] = w
        a = (_silu(_dot(u, wg)) * _dot(u, wu)).astype(BF16)
        return _dot(a, wd)

    @pl.when(live & (f == 0))
    def _():
        u = _adaln(x_ref[...], nw_ref[...], mod_ref)
        if with_ctx:
            u = jnp.concatenate([_adaln(xc_ref[...], nw_ref[...], modc_ref), u], axis=0)
        u_scr[...] = u
        acc_scr[...] = tile_product(u)

    @pl.when(live & (f > 0) & (f < nf - 1))
    def _():
        acc_scr[...] += tile_product(u_scr[...])

    @pl.when(live & (f == nf - 1))
    def _():
        acc = acc_scr[...] + tile_product(u_scr[...])
        h = x_ref[...] + 0.5 * mod_ref[2:3, :] * acc[n_ctx:]
        if final_norm:
            h = _rmsnorm(h, fnw_ref[...])
        o_ref[...] = h
        if with_ctx:
            oc_ref[...] = xc_ref[...] + 0.5 * modc_ref[2:3, :] * acc[:n_ctx]

    if with_head:
        @pl.when((i == 0) & (f == 0))
        def _():
            o_ref[...] = head_ref[...]


def _side_cast_spec(shape, n_blk, nf):
    rows, cols = shape
    assert rows % (BF16_SUBLANES * n_blk) == 0
    n_col = next(c for c in range(nf, 0, -1) if cols % c == 0 and (cols // c) % LANES == 0)
    return pl.BlockSpec((rows // n_blk, cols // n_col), lambda i, f: (i, jnp.minimum(f, n_col - 1)))


def _ffn_call(x, mod3, nw, wg, wu, wd, final_nw=None, side_casts=(), ctx=None, head=None, tm=512, tf=FFN_TF,
              emit_tf=None):
    t, d = x.shape
    dff = wd.shape[0]
    tm = min(tm, t)
    nf = dff // tf
    n_blk = 1 if ctx is not None else t // tm
    assert t % tm == 0 and dff % tf == 0 and nf >= 2
    final_norm, with_ctx, with_head = final_nw is not None, ctx is not None, head is not None
    emit_weights = wg.dtype == F32
    assert not emit_weights or n_blk == 1
    tile = (lambda i, f: jnp.where(i == 0, 0, f)) if with_head else (lambda i, f: f)
    flat = pl.BlockSpec((d, tf), lambda i, f: (0, tile(i, f)))
    tiled = pl.BlockSpec((None, d, tf), lambda i, f: (tile(i, f), 0, 0))
    down = pl.BlockSpec((tf, d), lambda i, f: (tile(i, f), 0))
    w_specs = [tiled if wg.ndim == 3 else flat, tiled if wu.ndim == 3 else flat, down]
    side_specs = [_side_cast_spec(a.shape, n_blk, nf) for a in side_casts]
    vec = lambda rows: pl.BlockSpec((rows, d), lambda i, f: (0, 0))
    in_specs = [pl.BlockSpec((tm, d), lambda i, f: (i, 0)), vec(3), vec(1)] + w_specs
    args = [x, mod3, nw, wg, wu, wd]
    if final_norm:
        in_specs.append(vec(1))
        args.append(final_nw)
    if with_ctx:
        in_specs += [vec(ctx[0].shape[0]), vec(3)]
        args += list(ctx)
    if with_head:
        in_specs.append(vec(tm))
        args.append(head)
    in_specs += side_specs
    args += list(side_casts)
    out_specs = [pl.BlockSpec((tm, d), lambda i, f: (i, 0))]
    out_shape = [jax.ShapeDtypeStruct((n_blk * tm, d), F32)]
    if with_ctx:
        out_specs.append(vec(ctx[0].shape[0]))
        out_shape.append(jax.ShapeDtypeStruct(ctx[0].shape, F32))
    if emit_weights:
        e_tf = emit_tf or tf
        assert e_tf % tf == 0
        sub = e_tf // tf
        emitted = pl.BlockSpec((None, d, tf), lambda i, f: (f // sub, 0, f % sub))
        out_specs += [emitted, emitted, down]
        out_shape += [jax.ShapeDtypeStruct((dff // e_tf, d, e_tf), BF16)] * 2 + [jax.ShapeDtypeStruct(wd.shape, BF16)]
    out_specs += side_specs
    out_shape += [jax.ShapeDtypeStruct(a.shape, BF16) for a in side_casts]
    rows = tm + (ctx[0].shape[0] if with_ctx else 0)
    return pl.pallas_call(
        functools.partial(_ffn_kernel, nf=nf, final_norm=final_norm, emit_weights=emit_weights,
                          n_side=len(side_casts), with_ctx=with_ctx, with_head=with_head),
        grid=(n_blk, nf),
        in_specs=in_specs,
        out_specs=out_specs,
        out_shape=out_shape,
        scratch_shapes=[pltpu.VMEM((rows, d), BF16), pltpu.VMEM((rows, d), F32)],
        compiler_params=_params("parallel", "arbitrary"),
        name="ffn",
    )(*args)


def _inproj_kernel(x_ref, mod_ref, nw_ref, w_ref, xc_ref, modc_ref, o_ref, oc_ref, u_scr, uc_scr):
    i, j = pl.program_id(0), pl.program_id(1)

    @pl.when(j == 0)
    def _():
        u = _adaln(x_ref[...], nw_ref[...], mod_ref)
        u_scr[...] = u
        o_ref[...] = _dot(u, w_ref[...])

    @pl.when(j > 0)
    def _():
        o_ref[...] = _dot(u_scr[...], w_ref[...])

    @pl.when((i == 0) & (j == 0))
    def _():
        uc_scr[...] = _adaln(xc_ref[...], nw_ref[...], modc_ref)

    @pl.when(i == 0)
    def _():
        oc_ref[...] = _dot(uc_scr[...], w_ref[...])


def _inproj_call(x, mod2, x_ctx, mod2_ctx, nw, w, tm=1024, tn=1024):
    t, d = x.shape
    t_ctx = x_ctx.shape[0]
    n = w.shape[1]
    tm = min(tm, t)
    n_col = n // tn
    assert t % tm == 0 and n % tn == 0
    vec = lambda rows: pl.BlockSpec((rows, d), lambda i, j: (0, 0))
    return pl.pallas_call(
        _inproj_kernel,
        grid=(t // tm, n_col),
        in_specs=[pl.BlockSpec((tm, d), lambda i, j: (i, 0)), vec(2), vec(1),
                  pl.BlockSpec((d, tn), lambda i, j: (0, j)), vec(t_ctx), vec(2)],
        out_specs=[pl.BlockSpec((tm, tn), lambda i, j: (i, j)),
                   pl.BlockSpec((t_ctx, tn), lambda i, j: (0, jnp.where(i == 0, j, n_col - 1)))],
        out_shape=[jax.ShapeDtypeStruct((t, n), F32), jax.ShapeDtypeStruct((t_ctx, n), F32)],
        scratch_shapes=[pltpu.VMEM((tm, d), BF16), pltpu.VMEM((t_ctx, d), BF16)],
        compiler_params=_params("arbitrary", "arbitrary"),
        name="inproj",
    )(x, mod2, nw, w, x_ctx, mod2_ctx)


def _split3(x):
    hi = x.astype(BF16)
    r = x - hi.astype(F32)
    mid = r.astype(BF16)
    lo = (r - mid.astype(F32)).astype(BF16)
    return hi, mid, lo


def _hg_gates(q_raw, f_raw, lb, tri):
    q = _silu(q_raw)
    f = lb + (1.0 - lb) * _sigmoid(f_raw)
    hi, mid, lo = _split3(jnp.log(f) * LOG2E)
    return q, 1.0 - f, f, _dot(tri, hi) + _dot(tri, mid) + _dot(tri, lo)


def _hg_level_operand(m, q, k, f, b, b_row, pos, rev):
    c = HG_CHUNK
    if m >= SUBLANES:
        qk, bmid = [], []
        for a in range(c // (2 * m)):
            lo_r, mid_r, hi_r = a * 2 * m, a * 2 * m + m, (a + 1) * 2 * m
            qk += [q[lo_r:mid_r], k[mid_r:hi_r]] if rev else [k[lo_r:mid_r], q[mid_r:hi_r]]
            bmid.append(jnp.broadcast_to(b_row(mid_r if rev else mid_r - 1), (2 * m, HEAD_DIM)))
        qk = jnp.concatenate(qk, axis=0)
        bmid = bmid[0] if len(bmid) == 1 else jnp.concatenate(bmid, axis=0)
        return (qk * jnp.exp2(-jnp.abs(b - bmid))).astype(BF16)
    half = pos & (2 * m - 1)
    q_side = (half < m) if rev else (half >= m)
    if m == 4:
        rows = [jnp.broadcast_to(b_row(a * 8 + (4 if rev else 3)), (8, HEAD_DIM)) for a in range(c // 8)]
        w = jnp.exp2(-jnp.abs(b - jnp.concatenate(rows, axis=0)))
    elif m == 2:
        f_next = pltpu.roll(f, c - 1, 0)
        f_prev = pltpu.roll(f, 1, 0)
        if rev:
            w = jnp.where(half == 0, f * f_next, jnp.where(half == 1, f, jnp.where(half == 2, 1.0, f_prev)))
        else:
            w = jnp.where(half == 0, f_next, jnp.where(half == 1, 1.0, jnp.where(half == 2, f, f * f_prev)))
    else:
        w = jnp.where(q_side, f, 1.0)
    return (jnp.where(q_side, q, k) * w).astype(BF16)


def _hg_state_step(st, k, vb, b, b_end):
    return st * jnp.exp2(b_end) + _dot_tn(vb, (k * jnp.exp2(b_end - b)).astype(BF16))


def _hg_ctx_kernel(q_ref, ff_ref, fb_ref, v_ref, lb_ref, tri_ref, o_ref, b_scr):
    c = HG_CHUNK
    n_chunks = q_ref.shape[0] // c
    for d, (f_ref, rev) in enumerate(((ff_ref, False), (fb_ref, True))):
        st = jnp.zeros((HEAD_DIM, HEAD_DIM), F32)
        for ci in (range(n_chunks - 1, -1, -1) if rev else range(n_chunks)):
            sl = pl.ds(ci * c, c)
            _, k, _, b = _hg_gates(q_ref[sl, :], f_ref[sl, :], lb_ref[d:d + 1, :], tri_ref[d])
            b_scr[...] = b
            st = _hg_state_step(st, k, v_ref[sl, :].astype(BF16), b, b_scr[pl.ds(0 if rev else c - 1, 1), :])
        o_ref[d] = st


def _hg_kernel(q_ref, f_ref, v_ref, st0_ref, lb_ref, tri_ref, lvl_ref, *rest, rev, heads_per_step, unroll,
               readout, n_side):
    rest = list(rest)
    of_ref, g_ref, nw_ref = (rest.pop(0), rest.pop(0), rest.pop(0)) if readout else (None, None, None)
    side_in = [rest.pop(0) for _ in range(n_side)]
    o_ref = rest.pop(0)
    side_out = [rest.pop(0) for _ in range(n_side)]
    b_scr, st_scr = rest
    c = HG_CHUNK

    for src, dst in zip(side_in, side_out):
        dst[...] = src[...].astype(BF16)

    @pl.when(pl.program_id(1) == 0)
    def _():
        st_scr[...] = st0_ref[...]

    n_chunks = q_ref.shape[0] // c
    tri = tri_ref[...]

    def body(i, carry):
        where = []
        for u in range(unroll):
            ci = i * unroll + u
            ci = (n_chunks - 1 - ci) if rev else ci
            rows = pl.ds(pl.multiple_of(ci * c, c), c)
            where += [(rows, slice(h * HEAD_DIM, (h + 1) * HEAD_DIM)) for h in range(heads_per_step)]
        n = len(where)
        gates = [_hg_gates(q_ref[r, cs], f_ref[r, cs], lb_ref[:, cs], tri) for r, cs in where]
        for p in range(n):
            b_scr[p] = gates[p][3]
        vbs = [v_ref[r, cs].astype(BF16) for r, cs in where]

        pos = lax.broadcasted_iota(jnp.int32, (c, HEAD_DIM), 0)
        lvl = lvl_ref[...]
        scores = [0.0] * n
        for m in HG_LEVELS:
            for p in range(n):
                q, k, f, b = gates[p]
                x = _hg_level_operand(m, q, k, f, b, lambda r, p=p: b_scr[p, pl.ds(r, 1), :], pos, rev)
                scores[p] = jnp.where(lvl == m, _dot_nt(x, x), scores[p])

        outs = [None] * n
        for h in range(heads_per_step):
            st = st_scr[h]
            for u in range(unroll):
                p = u * heads_per_step + h
                q, k, f, b = gates[p]
                outs[p] = _dot_nt((q * jnp.exp2(b)).astype(BF16), st.astype(BF16))
                st = _hg_state_step(st, k, vbs[p], b, b_scr[p, pl.ds(0 if rev else c - 1, 1), :])
            st_scr[h] = st

        for p, (r, cs) in enumerate(where):
            q, k, f, b = gates[p]
            o = outs[p] + _dot(scores[p].astype(BF16), vbs[p])
            o = o + jnp.sum(q * k, axis=-1, keepdims=True) * v_ref[r, cs]
            if readout:
                o = _rmsnorm(of_ref[r, cs] + o, nw_ref[...]) * _silu(g_ref[r, cs])
            o_ref[r, cs] = o.astype(o_ref.dtype)
        return carry

    lax.fori_loop(0, n_chunks // unroll, body, 0)


def _hg_tables(rev):
    t = np.arange(HG_CHUNK)[:, None]
    s = np.arange(HG_CHUNK)[None, :]
    x = t ^ s
    msb = np.where(x > 0, 2 ** np.floor(np.log2(np.maximum(x, 1))).astype(np.int64), 0)
    lvl = np.where((s > t) if rev else (s < t), msb, 0).astype(np.int32)
    tri = np.tril(np.ones((HG_CHUNK, HG_CHUNK), np.float32))
    return (tri.T if rev else tri), lvl


def _hg_ctx_call(p_ctx, lb, *, n_heads):
    t_ctx = p_ctx.shape[0]
    assert t_ctx % HG_CHUNK == 0
    tri = jnp.asarray(np.stack([_hg_tables(False)[0], _hg_tables(True)[0]]), BF16)
    col = lambda c0: pl.BlockSpec((t_ctx, HEAD_DIM), lambda h: (0, c0 + h))
    const = pl.BlockSpec((2, HG_CHUNK, HG_CHUNK), lambda h: (0, 0, 0))
    return pl.pallas_call(
        _hg_ctx_kernel,
        grid=(n_heads,),
        in_specs=[col(0), col(n_heads), col(2 * n_heads), col(3 * n_heads),
                  pl.BlockSpec((2, HEAD_DIM), lambda h: (0, h)), const],
        out_specs=pl.BlockSpec((2, None, HEAD_DIM, HEAD_DIM), lambda h: (0, h, 0, 0)),
        out_shape=jax.ShapeDtypeStruct((2, n_heads, HEAD_DIM, HEAD_DIM), F32),
        scratch_shapes=[pltpu.VMEM((HG_CHUNK, HEAD_DIM), F32)],
        compiler_params=_params("parallel"),
        name="hgrn_ctx",
    )(p_ctx, p_ctx, p_ctx, p_ctx, lb, tri)


def _hg_call(p_lat, st0, lb_dir, o_fw, norm_w, *, rev, n_heads, side_casts=(), tb=2048, heads_per_step=2,
             unroll=4):
    t_lat = p_lat.shape[0]
    tb = min(tb, t_lat)
    hb = heads_per_step
    assert t_lat % tb == 0 and tb % (HG_CHUNK * unroll) == 0 and n_heads % hb == 0
    n_blk, n_grp = t_lat // tb, n_heads // hb
    readout = o_fw is not None
    blk_row = (lambda j: n_blk - 1 - j) if rev else (lambda j: j)
    lat = lambda c0: pl.BlockSpec((tb, hb * HEAD_DIM), lambda g, j: (blk_row(j), c0 * n_grp + g))
    tri, lvl = _hg_tables(rev)
    const = pl.BlockSpec((HG_CHUNK, HG_CHUNK), lambda g, j: (0, 0))
    in_specs = [lat(0), lat(2 if rev else 1), lat(3),
                pl.BlockSpec((hb, HEAD_DIM, HEAD_DIM), lambda g, j: (g, 0, 0)),
                pl.BlockSpec((1, hb * HEAD_DIM), lambda g, j: (0, g)), const, const]
    args = [p_lat, p_lat, p_lat, st0, lb_dir, jnp.asarray(tri, BF16), jnp.asarray(lvl)]
    out_spec = pl.BlockSpec((tb, hb * HEAD_DIM), lambda g, j: (blk_row(j), g))
    if readout:
        in_specs += [out_spec, lat(4), pl.BlockSpec((1, HEAD_DIM), lambda g, j: (0, 0))]
        args += [o_fw, p_lat, norm_w]
    side_specs = [_side_cast_spec(a.shape, n_grp, n_blk) for a in side_casts]
    return pl.pallas_call(
        functools.partial(_hg_kernel, rev=rev, heads_per_step=hb, unroll=unroll, readout=readout,
                          n_side=len(side_casts)),
        grid=(n_grp, n_blk),
        in_specs=in_specs + side_specs,
        out_specs=[out_spec] + side_specs,
        out_shape=[jax.ShapeDtypeStruct((t_lat, n_heads * HEAD_DIM), BF16 if readout else F32)]
        + [jax.ShapeDtypeStruct(a.shape, BF16) for a in side_casts],
        scratch_shapes=[pltpu.VMEM((unroll * hb, HG_CHUNK, HEAD_DIM), F32),
                        pltpu.VMEM((hb, HEAD_DIM, HEAD_DIM), F32)],
        compiler_params=_params("parallel", "arbitrary"),
        name="hgrn_bwd" if rev else "hgrn_fwd",
    )(*args, *side_casts)


def _rope(x, cos, sin_signed, low):
    return x * cos + jnp.where(low, pltpu.roll(x, 96, 1), pltpu.roll(x, 32, 1)) * sin_signed


def _na_kernel(q_ref, k_ref, v_ref, kc_ref, vc_ref, rtab_ref, ctab_ref, cbias_ref,
               o_ref, kt_scr, v_scr, kct_scr, bias_scr, *, rows):
    tile = 2 * GRID_W
    n_rb = rows // NA_QROWS
    lane = lax.broadcasted_iota(jnp.int32, (GRID_W, HEAD_DIM), 1)
    low = (lane & (HEAD_DIM // 2 - 1)) < HEAD_DIM // 4

    def rope_rows(x_ref, row0, n):
        out = []
        for i in range(n):
            x = x_ref[pl.ds(pl.multiple_of((row0 + i) * GRID_W, GRID_W), GRID_W), :]
            cos = rtab_ref[0, pl.ds(row0 + i, 1), :] + ctab_ref[0]
            sin = rtab_ref[1, pl.ds(row0 + i, 1), :] + ctab_ref[1]
            out.append(_rope(x, cos, sin, low))
        return jnp.concatenate(out, axis=0)

    def prep(i, carry):
        js = [i * NA_PREP_TILES + u for u in range(NA_PREP_TILES)]
        roped = [rope_rows(k_ref, 2 * j, 2) for j in js]
        for j, kr in zip(js, roped):
            kt_scr[j] = kr.T.astype(BF16)
            v_scr[j] = v_ref[pl.ds(pl.multiple_of(j * tile, tile), tile), :].astype(BF16)
        return carry

    lax.fori_loop(0, rows // (2 * NA_PREP_TILES), prep, 0)
    kct_scr[...] = kc_ref[...].T.astype(BF16)

    bias_scr[...] = jnp.full(bias_scr.shape, NEG_BIG, F32)
    for e in range(3):
        for a in range(NA_QROWS):
            b0 = (0, a, NA_KROWS - NA_KH)[e]
            d0 = (NA_KH - 1 - a, NA_KH // 2 - 1, NA_QROWS - 1 - a)[e]
            for i in range(NA_KH):
                bias_scr[e, a * GRID_W:(a + 1) * GRID_W, (b0 + i) * GRID_W:(b0 + i + 1) * GRID_W] = (
                    cbias_ref[d0 + i] * LOG2E)

    vc = vc_ref[...].astype(BF16)

    def row_blocks(i, carry):
        rbs = [i * NA_STEP_BLOCKS + u for u in range(NA_STEP_BLOCKS)]
        qs = [(rope_rows(q_ref, rb * NA_QROWS, NA_QROWS) * (HEAD_DIM ** -0.5 * LOG2E)).astype(BF16) for rb in rbs]
        j0s = [jnp.clip(rb * (NA_QROWS // 2) - NA_KH // 4, 0, (rows - NA_KROWS) // 2) for rb in rbs]
        n_win = NA_KROWS * GRID_W
        s_w, s_c = [], []
        for rb, q, j0 in zip(rbs, qs, j0s):
            keys = jnp.concatenate([kt_scr[j0 + t] for t in range(NA_KROWS // 2)] + [kct_scr[...]], axis=1)
            edge = jnp.where(rb == 0, 0, jnp.where(rb == n_rb - 1, 2, 1))
            s = _dot(q, keys)
            s_w.append(s[:, :n_win] + bias_scr[edge])
            s_c.append(s[:, n_win:])
        nq = NA_QROWS * GRID_W
        for rb, j0, sw, sc in zip(rbs, j0s, s_w, s_c):
            m = jnp.maximum(jnp.max(sw, axis=-1, keepdims=True), jnp.max(sc, axis=-1, keepdims=True))
            p_w = jnp.exp2(sw - m)
            p_c = jnp.exp2(sc - m)
            denom = jnp.sum(p_w, axis=-1, keepdims=True) + jnp.sum(p_c, axis=-1, keepdims=True)
            p = jnp.concatenate([p_w.astype(BF16), p_c.astype(BF16)], axis=1)
            vals = jnp.concatenate([v_scr[j0 + t] for t in range(NA_KROWS // 2)] + [vc], axis=0)
            o_ref[pl.ds(pl.multiple_of(rb * nq, nq), nq), :] = (_dot(p, vals) / denom).astype(o_ref.dtype)
        return carry

    lax.fori_loop(0, n_rb // NA_STEP_BLOCKS, row_blocks, 0)


def _na_col_bias(rpb):
    col = np.arange(GRID_W)
    col_start = np.clip(col - NA_KW // 2, 0, GRID_W - NA_KW)
    col_in = (col[None, :] >= col_start[:, None]) & (col[None, :] < col_start[:, None] + NA_KW)
    dc = np.clip(col[None, :] - col[:, None], 1 - NA_KW, NA_KW - 1) + (NA_KW - 1)
    onehot = (dc[None] == np.arange(2 * NA_KW - 1)[:, None, None]).astype(np.float32)
    vals = jnp.einsum("hrk,kqc->hrqc", rpb.astype(F32), onehot, precision=lax.Precision.HIGHEST)
    return jnp.where(col_in[None, None], vals, NEG_BIG)


def _rope_tables(rows):
    quarter = HEAD_DIM // 4
    inv = ROPE_BASE ** (-jnp.arange(quarter, dtype=F32) / quarter)
    ang_r = jnp.arange(rows).astype(F32)[:, None] * inv[None, :]
    ang_c = jnp.arange(GRID_W).astype(F32)[:, None] * inv[None, :]
    half = lambda a: jnp.stack([jnp.concatenate([jnp.cos(a), jnp.cos(a)], axis=-1),
                                jnp.concatenate([-jnp.sin(a), jnp.sin(a)], axis=-1)])
    r, c = half(ang_r), half(ang_c)
    return jnp.concatenate([r, jnp.zeros_like(r)], axis=-1), jnp.concatenate([jnp.zeros_like(c), c], axis=-1)


def _na_call(p_ctx, p_lat, rpb, *, col0, n_heads):
    t_ctx, t = p_ctx.shape[0], p_lat.shape[0]
    rows = t // GRID_W
    assert t % GRID_W == 0 and rows % (NA_QROWS * NA_STEP_BLOCKS) == 0 and rows >= NA_KROWS + NA_QROWS
    assert rows % (2 * NA_PREP_TILES) == 0
    assert NA_QROWS % 2 == 0 and NA_KH % 4 == 0
    nq = NA_QROWS * GRID_W
    rtab, ctab = _rope_tables(rows)
    cbias = _na_col_bias(rpb)
    full = lambda col: pl.BlockSpec((t, HEAD_DIM), lambda h: (0, col + h))
    ctx = lambda col: pl.BlockSpec((t_ctx, HEAD_DIM), lambda h: (0, col + h))
    return pl.pallas_call(
        functools.partial(_na_kernel, rows=rows),
        grid=(n_heads,),
        in_specs=[full(col0), full(col0 + n_heads), full(col0 + 2 * n_heads),
                  ctx(col0 + n_heads), ctx(col0 + 2 * n_heads),
                  pl.BlockSpec((2, rows, HEAD_DIM), lambda h: (0, 0, 0)),
                  pl.BlockSpec((2, GRID_W, HEAD_DIM), lambda h: (0, 0, 0)),
                  pl.BlockSpec((None, 2 * NA_KH - 1, GRID_W, GRID_W), lambda h: (h, 0, 0, 0))],
        out_specs=pl.BlockSpec((t, HEAD_DIM), lambda h: (0, h)),
        out_shape=jax.ShapeDtypeStruct((t, n_heads * HEAD_DIM), BF16),
        scratch_shapes=[pltpu.VMEM((rows // 2, HEAD_DIM, 2 * GRID_W), BF16),
                        pltpu.VMEM((rows // 2, 2 * GRID_W, HEAD_DIM), BF16),
                        pltpu.VMEM((HEAD_DIM, t_ctx), BF16),
                        pltpu.VMEM((3, nq, NA_KROWS * GRID_W), F32)],
        compiler_params=_params("parallel"),
        name="natten",
    )(p_lat, p_lat, p_lat, p_ctx, p_ctx, rtab, ctab, cbias)


def _outproj_kernel(hg_ref, na_ref, h_ref, gate_ref, w_ref, o_ref):
    hw = hg_ref.shape[1]
    mix = _dot(hg_ref[...], w_ref[:hw, :]) + _dot(na_ref[...], w_ref[hw:, :])
    o_ref[...] = h_ref[...] + gate_ref[...] * mix


def _outproj_call(hg, na, h, gate, w, tm=512):
    t, d = h.shape
    hw, nw = hg.shape[1], na.shape[1]
    tm = min(tm, t)
    assert t % tm == 0
    return pl.pallas_call(
        _outproj_kernel,
        grid=(t // tm,),
        in_specs=[pl.BlockSpec((tm, hw), lambda i: (i, 0)),
                  pl.BlockSpec((tm, nw), lambda i: (i, 0)),
                  pl.BlockSpec((tm, d), lambda i: (i, 0)),
                  pl.BlockSpec((1, d), lambda i: (0, 0)),
                  pl.BlockSpec((hw + nw, d), lambda i: (0, 0))],
        out_specs=pl.BlockSpec((tm, d), lambda i: (i, 0)),
        out_shape=jax.ShapeDtypeStruct((t, d), F32),
        compiler_params=_params("parallel"),
        name="outproj",
    )(hg, na, h, gate, w)


def kernel(x, c, ctx, c_ctx, w_mod, b_mod, norm_w, ffn1_w_gate, ffn1_w_up, ffn1_w_down, w_in, hg_lb_logits,
           hg_norm_w, na_rpb, w_out, ffn2_w_gate, ffn2_w_up, ffn2_w_down, final_norm_w):
    batch, _, d = x.shape
    depth = w_mod.shape[0]
    assert batch == 1 and depth == 1
    hg_heads = hg_lb_logits.shape[-1] // HEAD_DIM
    na_heads = na_rpb.shape[1]
    row = lambda v: v.reshape(1, -1)

    cc = jnp.stack([c[0], c_ctx], axis=1)
    mods = _mod_call(cc, w_mod[0], row(b_mod[0]))
    m_l = mods[0].reshape(N_MOD, d)
    m_c = mods[1].reshape(N_MOD, d)
    lb = jnp.cumsum(jax.nn.softmax(hg_lb_logits.astype(F32), axis=1), axis=1)[:, 0]

    h_l0, h_c, *ffn1 = _ffn_call(x[0], m_l[0:3], row(norm_w[0, 0]), ffn1_w_gate[0], ffn1_w_up[0], ffn1_w_down[0],
                                 ctx=(ctx[0], m_c[0:3]), tf=FFN_HEAD_TF, emit_tf=FFN_TF)
    h_l, w_in_b = _ffn_call(x[0], m_l[0:3], row(norm_w[0, 0]), *ffn1, head=h_l0, side_casts=(w_in[0],))
    p_l, p_c = _inproj_call(h_l, m_l[3:5], h_c, m_c[3:5], row(norm_w[0, 1]), w_in_b)

    st0 = _hg_ctx_call(p_c, lb, n_heads=hg_heads)
    o_fw, wg2, w_out_b = _hg_call(p_l, st0[0], lb[0:1], None, None, rev=False, n_heads=hg_heads,
                                  side_casts=(ffn2_w_gate[0], w_out[0]))
    hg, wu2, wd2 = _hg_call(p_l, st0[1], lb[1:2], o_fw, row(hg_norm_w[0]), rev=True, n_heads=hg_heads,
                            side_casts=(ffn2_w_up[0], ffn2_w_down[0]))
    ffn2 = (wg2, wu2, wd2)
    na = _na_call(p_c, p_l, na_rpb[0], col0=5 * hg_heads, n_heads=na_heads)

    h_l = _outproj_call(hg, na, h_l, m_l[5:6], w_out_b)
    out, = _ffn_call(h_l, m_l[6:9], row(norm_w[0, 2]), *ffn2, final_nw=row(final_norm_w))
    return out[None]
```
